```python
import jax, jax.numpy as jnp
from jax import lax
import numpy as np

D_MODEL = 2048
BATCH = 1
SEQ = 16384
DEPTH = 2

MIX_WIDTH = D_MODEL
W_MLA = D_MODEL // 4
W_MOBA = D_MODEL // 4
W_CONV = D_MODEL // 4
W_POOL = MIX_WIDTH - W_MLA - W_MOBA - W_CONV

MLA_NOPE_DIM = 128
MLA_ROPE_DIM = 64
MLA_V_DIM = 128
MLA_QK_DIM = MLA_NOPE_DIM + MLA_ROPE_DIM
MLA_HEADS = W_MLA // MLA_V_DIM
Q_LORA = D_MODEL // 4
KV_LORA = D_MODEL // 8

MOBA_HEAD_DIM = 128
MOBA_HEADS = W_MOBA // MOBA_HEAD_DIM
MOBA_BLOCK = 256
MOBA_TOPK = 3
MOBA_ROT_DIM = MOBA_HEAD_DIM // 4

CONV_WIDTH = 31

POOL_WINDOWS = (2, 4, 8, 16)
POOL_GROUP = W_POOL // len(POOL_WINDOWS)

D_FF = -(-(8 * D_MODEL) // (3 * 256)) * 256

ROPE_THETA = 500000.0
Q_BLOCK = 128
NORM_EPS = 1e-5
NEG_INF = -1e30
DEEPNORM_ALPHA = (2 * DEPTH) ** 0.25
DEEPNORM_BETA = (8 * DEPTH) ** -0.25

IN_SIZES = (Q_LORA, KV_LORA, MLA_ROPE_DIM, W_MOBA, W_MOBA, W_MOBA, 2 * W_CONV, W_POOL)
IN_COLS = Q_LORA + KV_LORA + MLA_ROPE_DIM + 3 * W_MOBA + 2 * W_CONV + W_POOL

kernel_name = 'hybrid_parallel_mla_moba_conv_pool'


def _split_columns(h, sizes):
    outs = []
    start = 0
    for n in sizes:
        outs.append(h[..., start:start + n])
        start += n
    return outs


def _rope_tables(positions, dim):
    inv_freq = 1.0 / (ROPE_THETA ** (jnp.arange(0, dim, 2, dtype=jnp.float32) / dim))
    ang = positions.astype(jnp.float32)[..., None] * inv_freq
    return jnp.cos(ang), jnp.sin(ang)


def _apply_rope(x, cos, sin):
    half = x.shape[-1] // 2
    xf = x.astype(jnp.float32)
    c = cos[:, :, None, :]
    s = sin[:, :, None, :]
    x1, x2 = xf[..., :half], xf[..., half:]
    return jnp.concatenate([x1 * c - x2 * s, x1 * s + x2 * c], axis=-1).astype(x.dtype)


def _partial_rope(x, cos, sin):
    return jnp.concatenate([_apply_rope(x[..., :MOBA_ROT_DIM], cos, sin), x[..., MOBA_ROT_DIM:]], axis=-1)


def _layer_norm(x, g, b):
    xf = x.astype(jnp.float32)
    mu = jnp.mean(xf, axis=-1, keepdims=True)
    xc = xf - mu
    var = jnp.mean(xc * xc, axis=-1, keepdims=True)
    y = xc * lax.rsqrt(var + NORM_EPS) * g.astype(jnp.float32) + b.astype(jnp.float32)
    return y.astype(x.dtype)


def _rms_norm(x, g):
    xf = x.astype(jnp.float32)
    y = xf * lax.rsqrt(jnp.mean(xf * xf, axis=-1, keepdims=True) + NORM_EPS) * g.astype(jnp.float32)
    return y.astype(x.dtype)


def _causal_attention(q, k, v, scale):
    b, s, h, dk = q.shape
    nq = s // Q_BLOCK
    q_blocks = q.reshape(b, nq, Q_BLOCK, h, dk).swapaxes(0, 1)
    k_pos = jnp.arange(s)

    def block(args):
        q_i, i = args
        scores = jnp.einsum('bqhd,bkhd->bhqk', q_i, k, preferred_element_type=jnp.float32) * scale
        q_pos = i * Q_BLOCK + jnp.arange(Q_BLOCK)
        scores = jnp.where(k_pos[None, :] <= q_pos[:, None], scores, NEG_INF)
        p = jax.nn.softmax(scores, axis=-1).astype(v.dtype)
        return jnp.einsum('bhqk,bkhd->bqhd', p, v)

    out = lax.map(block, (q_blocks, jnp.arange(nq)))
    return out.swapaxes(0, 1).reshape(b, s, h, v.shape[-1])


def _moba_attention(q, k, v):
    b, s, h, d = q.shape
    scale = d ** -0.5
    n_blk = max(-(-s // MOBA_BLOCK), MOBA_TOPK)
    pad = n_blk * MOBA_BLOCK - s
    kh = jnp.pad(k, ((0, 0), (0, pad), (0, 0), (0, 0))).swapaxes(1, 2)
    vh = jnp.pad(v, ((0, 0), (0, pad), (0, 0), (0, 0))).swapaxes(1, 2)
    k_mean = jnp.mean(kh.astype(jnp.float32).reshape(b, h, n_blk, MOBA_BLOCK, d), axis=3)
    nq = s // Q_BLOCK
    q_blocks = q.swapaxes(1, 2).reshape(b, h, nq, Q_BLOCK, d).transpose(2, 0, 1, 3, 4)
    b_idx = jnp.arange(b)[:, None, None, None]
    h_idx = jnp.arange(h)[None, :, None, None]
    offs = jnp.arange(MOBA_BLOCK)
    blk_ids = jnp.arange(n_blk)
    n_sel = MOBA_TOPK * MOBA_BLOCK

    def block(args):
        q_i, i = args
        q_start = i * Q_BLOCK
        own = q_start // MOBA_BLOCK
        gate = jnp.einsum('bhqd,bhnd->bhqn', q_i.astype(jnp.float32), k_mean)
        gate = jnp.where((blk_ids < own)[None, None, None, :], gate, NEG_INF)
        _, sel = lax.top_k(gate, MOBA_TOPK)
        sel_ok = jnp.repeat(sel < own, MOBA_BLOCK, axis=-1)
        pos = (sel[..., None] * MOBA_BLOCK + offs).reshape(b, h, Q_BLOCK, n_sel)
        k_sel = kh[b_idx, h_idx, pos]
        v_sel = vh[b_idx, h_idx, pos]
        s_sel = jnp.einsum('bhqd,bhqkd->bhqk', q_i, k_sel, preferred_element_type=jnp.float32) * scale
        s_sel = jnp.where(sel_ok, s_sel, NEG_INF)
        own_start = own * MOBA_BLOCK
        k_own = lax.dynamic_slice_in_dim(kh, own_start, MOBA_BLOCK, axis=2)
        v_own = lax.dynamic_slice_in_dim(vh, own_start, MOBA_BLOCK, axis=2)
        s_own = jnp.einsum('bhqd,bhkd->bhqk', q_i, k_own, preferred_element_type=jnp.float32) * scale
        q_pos = q_start + jnp.arange(Q_BLOCK)
        k_pos = own_start + offs
        s_own = jnp.where(k_pos[None, :] <= q_pos[:, None], s_own, NEG_INF)
        p = jax.nn.softmax(jnp.concatenate([s_sel, s_own], axis=-1), axis=-1).astype(v.dtype)
        return (jnp.einsum('bhqk,bhqkd->bhqd', p[..., :n_sel], v_sel)
                + jnp.einsum('bhqk,bhkd->bhqd', p[..., n_sel:], v_own))

    out = lax.map(block, (q_blocks, jnp.arange(nq)))
    return out.transpose(1, 0, 3, 2, 4).reshape(b, s, h, d)


def _conformer_conv(u, dw_w, dw_b, ln_g, ln_b, pw_w):
    a, g = jnp.split(u, 2, axis=-1)
    hdn = a * jax.nn.sigmoid(g)
    c = hdn.shape[-1]
    hp = jnp.pad(hdn, ((0, 0), (CONV_WIDTH - 1, 0), (0, 0)))
    y = lax.conv_general_dilated(hp, dw_w[:, None, :].astype(hdn.dtype), (1,), 'VALID',
                                 dimension_numbers=('NWC', 'WIO', 'NWC'),
                                 feature_group_count=c) + dw_b
    y = jax.nn.silu(_layer_norm(y, ln_g, ln_b))
    return y @ pw_w


def _multiscale_pool(u, w_pool, scale):
    b, s, c = u.shape
    uf = u.astype(jnp.float32)
    csum = jnp.cumsum(uf, axis=1)
    t = jnp.arange(s)
    groups = []
    for gi, w in enumerate(POOL_WINDOWS):
        sl = slice(gi * POOL_GROUP, (gi + 1) * POOL_GROUP)
        cg = csum[..., sl]
        lower = jnp.pad(cg[:, :s - w], ((0, 0), (w, 0), (0, 0)))
        count = jnp.minimum(t + 1, w).astype(jnp.float32)[None, :, None]
        groups.append((cg - lower) / count - uf[..., sl])
    pooled = jnp.stack(groups, axis=2).astype(u.dtype)
    mixed = jnp.einsum('bsgc,gcd->bsgd', pooled, w_pool).reshape(b, s, c)
    return mixed * scale


def _hybrid_layer(x, cos_mla, sin_mla, cos_moba, sin_moba, w_in, g_q, w_uq, g_kv, w_ukv,
                  conv_dw_w, conv_dw_b, conv_ln_g, conv_ln_b, conv_pw_w, pool_w, pool_scale,
                  w_out, ln1_g, ln1_b, w_gate, w_up, w_down, ln2_g, ln2_b):
    b, s, _ = x.shape
    h = x @ w_in
    c_q, c_kv, k_rope, mq, mk, mv, conv_in, pool_in = _split_columns(h, IN_SIZES)

    q = (_rms_norm(c_q, g_q) @ w_uq).reshape(b, s, MLA_HEADS, MLA_QK_DIM)
    q_pe = _apply_rope(q[..., MLA_NOPE_DIM:], cos_mla, sin_mla)
    q_a = jnp.concatenate([q[..., :MLA_NOPE_DIM], q_pe], axis=-1)
    kv = (_rms_norm(c_kv, g_kv) @ w_ukv).reshape(b, s, MLA_HEADS, MLA_NOPE_DIM + MLA_V_DIM)
    k_pe = _apply_rope(k_rope[:, :, None, :], cos_mla, sin_mla)
    k_a = jnp.concatenate([kv[..., :MLA_NOPE_DIM],
                           jnp.broadcast_to(k_pe, (b, s, MLA_HEADS, MLA_ROPE_DIM))], axis=-1)
    o_mla = _causal_attention(q_a, k_a, kv[..., MLA_NOPE_DIM:], MLA_QK_DIM ** -0.5).reshape(b, s, W_MLA)

    q_b = _partial_rope(mq.reshape(b, s, MOBA_HEADS, MOBA_HEAD_DIM), cos_moba, sin_moba)
    k_b = _partial_rope(mk.reshape(b, s, MOBA_HEADS, MOBA_HEAD_DIM), cos_moba, sin_moba)
    v_b = mv.reshape(b, s, MOBA_HEADS, MOBA_HEAD_DIM)
    o_moba = _moba_attention(q_b, k_b, v_b).reshape(b, s, W_MOBA)

    o_conv = _conformer_conv(conv_in, conv_dw_w, conv_dw_b, conv_ln_g, conv_ln_b, conv_pw_w)

    o_pool = _multiscale_pool(pool_in, pool_w, pool_scale)

    mix = jnp.concatenate([o_mla, o_moba, o_conv, o_pool], axis=-1) @ w_out
    x = _layer_norm(DEEPNORM_ALPHA * x + mix, ln1_g, ln1_b)

    ffn = (jax.nn.silu(x @ w_gate) * (x @ w_up)) @ w_down
    return _layer_norm(DEEPNORM_ALPHA * x + ffn, ln2_g, ln2_b)


def _normal(k, shape, std):
    return jax.random.normal(k, shape, jnp.float32) * std


def setup_inputs(seed: int = 0) -> dict:
    key = jax.random.key(seed)
    ks = jax.random.split(key, 24)
    L = DEPTH
    x = jax.random.normal(ks[0], (BATCH, SEQ, D_MODEL), jnp.float32)
    offset = jax.random.randint(ks[1], (BATCH, 1), 0, 4096, dtype=jnp.int32)
    positions = offset + jnp.arange(SEQ, dtype=jnp.int32)[None, :]
    return {
        'x': x,
        'positions': positions,
        'w_in': _normal(ks[2], (L, D_MODEL, IN_COLS), D_MODEL ** -0.5),
        'g_q': 1.0 + _normal(ks[3], (L, Q_LORA), 0.02),
        'w_uq': _normal(ks[4], (L, Q_LORA, MLA_HEADS * MLA_QK_DIM), Q_LORA ** -0.5),
        'g_kv': 1.0 + _normal(ks[5], (L, KV_LORA), 0.02),
        'w_ukv': _normal(ks[6], (L, KV_LORA, MLA_HEADS * (MLA_NOPE_DIM + MLA_V_DIM)), KV_LORA ** -0.5),
        'conv_dw_w': _normal(ks[7], (L, CONV_WIDTH, W_CONV), CONV_WIDTH ** -0.5),
        'conv_dw_b': _normal(ks[8], (L, W_CONV), 0.02),
        'conv_ln_g': 1.0 + _normal(ks[9], (L, W_CONV), 0.02),
        'conv_ln_b': _normal(ks[10], (L, W_CONV), 0.02),
        'conv_pw_w': _normal(ks[11], (L, W_CONV, W_CONV), W_CONV ** -0.5),
        'pool_w': _normal(ks[12], (L, len(POOL_WINDOWS), POOL_GROUP, POOL_GROUP), POOL_GROUP ** -0.5),
        'pool_scale': 1.0 + _normal(ks[13], (L, W_POOL), 0.1),
        'w_out': _normal(ks[14], (L, MIX_WIDTH, D_MODEL), MIX_WIDTH ** -0.5 * DEEPNORM_BETA),
        'ln1_g': 1.0 + _normal(ks[15], (L, D_MODEL), 0.02),
        'ln1_b': _normal(ks[16], (L, D_MODEL), 0.02),
        'w_gate': _normal(ks[17], (L, D_MODEL, D_FF), D_MODEL ** -0.5),
        'w_up': _normal(ks[18], (L, D_MODEL, D_FF), D_MODEL ** -0.5),
        'w_down': _normal(ks[19], (L, D_FF, D_MODEL), D_FF ** -0.5 * DEEPNORM_BETA),
        'ln2_g': 1.0 + _normal(ks[20], (L, D_MODEL), 0.02),
        'ln2_b': _normal(ks[21], (L, D_MODEL), 0.02),
    }


def reference(x, positions, w_in, g_q, w_uq, g_kv, w_ukv, conv_dw_w, conv_dw_b, conv_ln_g,
              conv_ln_b, conv_pw_w, pool_w, pool_scale, w_out, ln1_g, ln1_b, w_gate, w_up,
              w_down, ln2_g, ln2_b):
    cos_mla, sin_mla = _rope_tables(positions, MLA_ROPE_DIM)
    cos_moba, sin_moba = _rope_tables(positions, MOBA_ROT_DIM)
    h = x
    for l in range(DEPTH):
        h = _hybrid_layer(h, cos_mla, sin_mla, cos_moba, sin_moba, w_in[l], g_q[l], w_uq[l],
                          g_kv[l], w_ukv[l], conv_dw_w[l], conv_dw_b[l], conv_ln_g[l],
                          conv_ln_b[l], conv_pw_w[l], pool_w[l], pool_scale[l], w_out[l],
                          ln1_g[l], ln1_b[l], w_gate[l], w_up[l], w_down[l], ln2_g[l], ln2_b[l])
    return h
```

```python
import functools

import jax
import jax.numpy as jnp
from jax import lax
from jax.experimental import pallas as pl
from jax.experimental.pallas import tpu as pltpu

F32 = jnp.float32
BF16 = jnp.bfloat16

D_MODEL = 2048
W_GROUP = 512
MLA_HEADS = 4
MLA_NOPE = 128
MLA_ROPE = 64
MLA_V = 128
MLA_QK = MLA_NOPE + MLA_ROPE
Q_LORA = 512
KV_LORA = 256
MOBA_HEADS = 4
MOBA_DIM = 128
MOBA_BLOCK = 256
MOBA_TOPK = 3
MOBA_ROT = 32
CONV_WIDTH = 31
POOL_WINDOWS = (2, 4, 8, 16)
POOL_GROUP = 128
ROPE_THETA = 500000.0
NORM_EPS = 1e-5
NEG_INF = -1e30
DEPTH = 2
DEEPNORM_ALPHA = (2 * DEPTH) ** 0.25

LANES = 128
HEADS = MLA_HEADS + MOBA_HEADS
QK_PAD = 256
LOG2E = 1.4426950408889634
VMEM_LIMIT = 56 * 1024 * 1024

COL_CONV_A = 0
COL_CONV_G = 512
COL_CQ = 1024
COL_POOL = 1536
COL_MQ = 2048
COL_MK = 2560
COL_MV = 3072
COL_CKV = 3584
COL_KROPE = 3840
IN_PAD = 4096


def _cparams(sem):
    return pltpu.CompilerParams(dimension_semantics=sem, vmem_limit_bytes=VMEM_LIMIT)


def _matmul_kernel(x_ref, w_ref, o_ref):
    o_ref[...] = jnp.dot(x_ref[...].astype(BF16), w_ref[...],
                         preferred_element_type=F32).astype(o_ref.dtype)


def _matmul(x, w, out_dtype, tm, tn):
    m, k = x.shape
    n = w.shape[1]
    return pl.pallas_call(
        _matmul_kernel,
        grid=(m // tm, n // tn),
        in_specs=[pl.BlockSpec((tm, k), lambda i, j: (i, 0)),
                  pl.BlockSpec((k, tn), lambda i, j: (0, j))],
        out_specs=pl.BlockSpec((tm, tn), lambda i, j: (i, j)),
        out_shape=jax.ShapeDtypeStruct((m, n), out_dtype),
        compiler_params=_cparams(("parallel", "parallel")),
        name="in_proj",
    )(x, w)


def _rms(x, g):
    return x * lax.rsqrt(jnp.mean(x * x, axis=-1, keepdims=True) + NORM_EPS) * g


def _rope128(r, c, s):
    return r * c + pltpu.roll(r, 64, 1) * s


def _prep_attn_kernel(cq_ref, mq_ref, mk_ref, mv_ref, ckv_ref, kr_ref,
                      gq_ref, gkv_ref, wuq_ref, wukv_ref,
                      ca_ref, sa_ref, cm_ref, sm_ref,
                      q8_ref, k8_ref, v8_ref, kmean_ref, *, tr):
    i = pl.program_id(0)

    @pl.when(i == 0)
    def _():
        kmean_ref[...] = jnp.zeros_like(kmean_ref)

    ca = ca_ref[...]
    sa = sa_ref[...]
    cm = cm_ref[...]
    sm = sm_ref[...]
    sc_a = MLA_QK ** -0.5 * LOG2E
    sc_b = MOBA_DIM ** -0.5 * LOG2E

    qn = _rms(cq_ref[...], gq_ref[...]).astype(BF16)
    q = jnp.dot(qn, wuq_ref[...], preferred_element_type=F32)
    for h in range(MLA_HEADS):
        base = h * QK_PAD
        q8_ref[h, :, 0:LANES] = (q[:, base:base + LANES] * sc_a).astype(BF16)
        pe = _rope128(q[:, base + LANES:base + 2 * LANES], ca, sa)
        q8_ref[h, :, LANES:2 * LANES] = (pe * sc_a).astype(BF16)

    kvn = _rms(ckv_ref[...], gkv_ref[...]).astype(BF16)
    kv = jnp.dot(kvn, wukv_ref[...], preferred_element_type=F32)
    kpe = _rope128(kr_ref[...], ca, sa).astype(BF16)
    for h in range(MLA_HEADS):
        base = h * 2 * LANES
        k8_ref[h, :, 0:LANES] = kv[:, base:base + LANES].astype(BF16)
        k8_ref[h, :, LANES:2 * LANES] = kpe
        v8_ref[h] = kv[:, base + LANES:base + 2 * LANES].astype(BF16)

    row = i * tr + lax.broadcasted_iota(jnp.int32, (tr, 1), 0)
    own = lax.shift_right_logical(row, 8)
    lane = lax.broadcasted_iota(jnp.int32, (tr, LANES), 1).astype(F32)
    own = own.astype(F32)
    valid = lane < own
    is_own = lane == own
    onehot = jnp.where(is_own, 1.0, 0.0).astype(BF16)
    blocks_per_tile = tr // MOBA_BLOCK
    for h in range(MOBA_HEADS):
        sl = slice(h * LANES, (h + 1) * LANES)
        kb = _rope128(mk_ref[:, sl], cm, sm)
        qb = _rope128(mq_ref[:, sl], cm, sm)
        for b in range(blocks_per_tile):
            ksum = jnp.sum(kb[b * MOBA_BLOCK:(b + 1) * MOBA_BLOCK], axis=0, keepdims=True)
            kmean_ref[h, pl.ds(i * blocks_per_tile + b, 1), :] = ksum * (1.0 / MOBA_BLOCK)
        gate = lax.dot_general(qb, kmean_ref[h], (((1,), (1,)), ((), ())),
                               precision=lax.Precision.HIGHEST,
                               preferred_element_type=F32)
        g = jnp.where(valid, gate, NEG_INF)
        sel = jnp.zeros((tr, LANES), jnp.bool_)
        for _ in range(MOBA_TOPK):
            mx = jnp.max(g, axis=1, keepdims=True)
            idx = jnp.min(jnp.where(g == mx, lane, float(LANES)), axis=1, keepdims=True)
            hit = lane == idx
            sel = jnp.logical_or(sel, hit)
            g = jnp.where(hit, -3.0e38, g)
        ok = jnp.logical_or(jnp.logical_and(sel, valid), is_own)
        bias = jnp.where(ok, 0.0, NEG_INF)
        q8_ref[MLA_HEADS + h, :, 0:LANES] = (qb * sc_b).astype(BF16)
        q8_ref[MLA_HEADS + h, :, LANES:2 * LANES] = bias.astype(BF16)
        k8_ref[MLA_HEADS + h, :, 0:LANES] = kb.astype(BF16)
        k8_ref[MLA_HEADS + h, :, LANES:2 * LANES] = onehot
        v8_ref[MLA_HEADS + h] = mv_ref[:, sl].astype(BF16)


def _prep_attn(h, g_q, g_kv, w_uq_p, w_ukv, ca, sa, cm, sm, tr):
    s = h.shape[0]
    hb = lambda w, c: pl.BlockSpec((tr, w), lambda i, c=c: (i, c))
    full = lambda a: pl.BlockSpec(a.shape, lambda i: (0,) * a.ndim)
    tab = pl.BlockSpec((tr, LANES), lambda i: (i, 0))
    out3 = lambda w: pl.BlockSpec((HEADS, tr, w), lambda i: (0, i, 0))
    return pl.pallas_call(
        functools.partial(_prep_attn_kernel, tr=tr),
        grid=(s // tr,),
        in_specs=[hb(512, COL_CQ // 512), hb(512, COL_MQ // 512), hb(512, COL_MK // 512),
                  hb(512, COL_MV // 512), hb(256, COL_CKV // 256), hb(128, COL_KROPE // 128),
                  full(g_q), full(g_kv), full(w_uq_p), full(w_ukv), tab, tab, tab, tab],
        out_specs=[out3(QK_PAD), out3(QK_PAD), out3(LANES)],
        out_shape=[jax.ShapeDtypeStruct((HEADS, s, QK_PAD), BF16),
                   jax.ShapeDtypeStruct((HEADS, s, QK_PAD), BF16),
                   jax.ShapeDtypeStruct((HEADS, s, LANES), BF16)],
        scratch_shapes=[pltpu.VMEM((MOBA_HEADS, LANES, LANES), F32)],
        compiler_params=_cparams(("arbitrary",)),
        name="prep_attn",
    )(h, h, h, h, h, h, g_q, g_kv, w_uq_p, w_ukv, ca, sa, cm, sm)


CONV_HALO = 32
POOL_HALO = 16


def _prep_mix_kernel(a_ref, g_ref, p_ref, dww_ref, dwb_ref, lng_ref, lnb_ref, pww_ref,
                     poolw_ref, pscale_ref, o_ref, hbuf, pbuf, *, tr):
    i = pl.program_id(0)

    @pl.when(i == 0)
    def _():
        hbuf[0:CONV_HALO, :] = jnp.zeros((CONV_HALO, W_GROUP), F32)
        pbuf[0:POOL_HALO, :] = jnp.zeros((POOL_HALO, W_GROUP), F32)

    @pl.when(i > 0)
    def _():
        hbuf[0:CONV_HALO, :] = hbuf[tr:tr + CONV_HALO, :]
        pbuf[0:POOL_HALO, :] = pbuf[tr:tr + POOL_HALO, :]

    hdn = a_ref[...] * jax.nn.sigmoid(g_ref[...])
    hbuf[CONV_HALO:CONV_HALO + tr, :] = hdn
    y = jnp.zeros((tr, W_GROUP), F32) + dwb_ref[...]
    first = CONV_HALO - (CONV_WIDTH - 1)
    for j in range(CONV_WIDTH):
        y = y + dww_ref[j:j + 1, :] * hbuf[first + j:first + j + tr, :]
    mu = jnp.mean(y, axis=-1, keepdims=True)
    yc = y - mu
    var = jnp.mean(yc * yc, axis=-1, keepdims=True)
    z = yc * lax.rsqrt(var + NORM_EPS) * lng_ref[...] + lnb_ref[...]
    z = z * jax.nn.sigmoid(z)
    o_ref[:, 0:W_GROUP] = jnp.dot(z.astype(BF16), pww_ref[...],
                                  preferred_element_type=F32).astype(o_ref.dtype)

    u = p_ref[...]
    pbuf[POOL_HALO:POOL_HALO + tr, :] = u
    t = i * tr + lax.broadcasted_iota(jnp.int32, (tr, 1), 0)
    for gi, w in enumerate(POOL_WINDOWS):
        sl = slice(gi * POOL_GROUP, (gi + 1) * POOL_GROUP)
        ssum = u[:, sl]
        for d in range(1, w):
            ssum = ssum + pbuf[POOL_HALO - d:POOL_HALO - d + tr, sl]
        count = jnp.minimum(t + 1, w).astype(F32)
        pooled = ssum / count - u[:, sl]
        mixed = jnp.dot(pooled.astype(BF16), poolw_ref[gi], preferred_element_type=F32)
        o_ref[:, W_GROUP + gi * POOL_GROUP:W_GROUP + (gi + 1) * POOL_GROUP] = (
            mixed * pscale_ref[:, sl]).astype(o_ref.dtype)


def _prep_mix(h, dw_w, dw_b, ln_g, ln_b, pw_w, pool_w, pool_scale, tr):
    s = h.shape[0]
    hb = lambda c: pl.BlockSpec((tr, 512), lambda i, c=c: (i, c))
    full = lambda a: pl.BlockSpec(a.shape, lambda i: (0,) * a.ndim)
    return pl.pallas_call(
        functools.partial(_prep_mix_kernel, tr=tr),
        grid=(s // tr,),
        in_specs=[hb(COL_CONV_A // 512), hb(COL_CONV_G // 512), hb(COL_POOL // 512),
                  full(dw_w), full(dw_b), full(ln_g), full(ln_b), full(pw_w),
                  full(pool_w), full(pool_scale)],
        out_specs=pl.BlockSpec((tr, 2 * W_GROUP), lambda i: (i, 0)),
        out_shape=jax.ShapeDtypeStruct((s, 2 * W_GROUP), BF16),
        scratch_shapes=[pltpu.VMEM((CONV_HALO + tr, W_GROUP), F32),
                        pltpu.VMEM((POOL_HALO + tr, W_GROUP), F32)],
        compiler_params=_cparams(("arbitrary",)),
        name="prep_mix",
    )(h, h, h, dw_w, dw_b, ln_g, ln_b, pw_w, pool_w, pool_scale)


def _attn_kernel(q_ref, k_ref, v_ref, o_ref, *, tq, tk):
    i = pl.program_id(1)
    q = q_ref[0]

    def step(j, carry, masked):
        m, l, acc = carry
        start = pl.multiple_of(j * tk, tk)
        k = k_ref[0, pl.ds(start, tk), :]
        v = v_ref[0, pl.ds(start, tk), :]
        s = lax.dot_general(q, k, (((1,), (1,)), ((), ())), preferred_element_type=F32)
        if masked:
            rq = i * tq + lax.broadcasted_iota(jnp.int32, (tq, tk), 0)
            ck = j * tk + lax.broadcasted_iota(jnp.int32, (tq, tk), 1)
            s = jnp.where(ck <= rq, s, NEG_INF)
        m_new = jnp.maximum(m, jnp.max(s, axis=1, keepdims=True))
        alpha = jnp.exp2(m - m_new)
        p = jnp.exp2(s - m_new)
        l = alpha * l + jnp.sum(p, axis=1, keepdims=True)
        acc = alpha * acc + jnp.dot(p.astype(BF16), v, preferred_element_type=F32)
        return m_new, l, acc

    init = (jnp.full((tq, 1), NEG_INF, F32), jnp.zeros((tq, 1), F32),
            jnp.zeros((tq, LANES), F32))
    n_full = i * (tq // tk)
    carry = lax.fori_loop(0, n_full, functools.partial(step, masked=False), init)
    for d in range(tq // tk):
        carry = step(n_full + d, carry, masked=True)
    _, l, acc = carry
    o_ref[...] = (acc / l).astype(o_ref.dtype)


def _attention(q8, k8, v8, tq, tk):
    nh, s, _ = q8.shape
    return pl.pallas_call(
        functools.partial(_attn_kernel, tq=tq, tk=tk),
        grid=(nh, s // tq),
        in_specs=[pl.BlockSpec((1, tq, QK_PAD), lambda h, i: (h, i, 0)),
                  pl.BlockSpec((1, s, QK_PAD), lambda h, i: (h, 0, 0)),
                  pl.BlockSpec((1, s, LANES), lambda h, i: (h, 0, 0))],
        out_specs=pl.BlockSpec((tq, LANES), lambda h, i: (i, h)),
        out_shape=jax.ShapeDtypeStruct((s, nh * LANES), BF16),
        compiler_params=_cparams(("parallel", "parallel")),
        name="attention",
    )(q8, k8, v8)


def _layer_norm(y, g, b):
    mu = jnp.mean(y, axis=-1, keepdims=True)
    yc = y - mu
    var = jnp.mean(yc * yc, axis=-1, keepdims=True)
    return yc * lax.rsqrt(var + NORM_EPS) * g + b


def _out_proj_kernel(att_ref, mix_ref, x_ref, wa_ref, wm_ref, g_ref, b_ref, o_ref):
    y = jnp.dot(att_ref[...], wa_ref[...], preferred_element_type=F32)
    y = y + jnp.dot(mix_ref[...], wm_ref[...], preferred_element_type=F32)
    y = y + DEEPNORM_ALPHA * x_ref[...]
    o_ref[...] = _layer_norm(y, g_ref[...], b_ref[...])


def _out_proj(att, mix, x, w_att, w_mix, ln_g, ln_b, tm):
    s, d = x.shape
    row = lambda w: pl.BlockSpec((tm, w), lambda i: (i, 0))
    full = lambda a: pl.BlockSpec(a.shape, lambda i: (0,) * a.ndim)
    return pl.pallas_call(
        _out_proj_kernel,
        grid=(s // tm,),
        in_specs=[row(att.shape[1]), row(mix.shape[1]), row(d), full(w_att), full(w_mix),
                  full(ln_g), full(ln_b)],
        out_specs=row(d),
        out_shape=jax.ShapeDtypeStruct((s, d), F32),
        compiler_params=_cparams(("parallel",)),
        name="out_proj_ln",
    )(att, mix, x, w_att, w_mix, ln_g, ln_b)


def _ffn_kernel(x_ref, wg_ref, wu_ref, wd_ref, g_ref, b_ref, o_ref, xb_ref):
    f = pl.program_id(1)

    @pl.when(f == 0)
    def _():
        x = x_ref[...]
        xb_ref[...] = x.astype(BF16)
        o_ref[...] = DEEPNORM_ALPHA * x

    xb = xb_ref[...]
    gate = jnp.dot(xb, wg_ref[...], preferred_element_type=F32)
    up = jnp.dot(xb, wu_ref[...], preferred_element_type=F32)
    act = (gate * jax.nn.sigmoid(gate) * up).astype(BF16)
    o_ref[...] += jnp.dot(act, wd_ref[...], preferred_element_type=F32)

    @pl.when(f == pl.num_programs(1) - 1)
    def _():
        o_ref[...] = _layer_norm(o_ref[...], g_ref[...], b_ref[...])


def _ffn(x, w_gate, w_up, w_down, ln_g, ln_b, tm, tf):
    s, d = x.shape
    dff = w_gate.shape[1]
    return pl.pallas_call(
        _ffn_kernel,
        grid=(s // tm, dff // tf),
        in_specs=[pl.BlockSpec((tm, d), lambda i, f: (i, 0)),
                  pl.BlockSpec((d, tf), lambda i, f: (0, f)),
                  pl.BlockSpec((d, tf), lambda i, f: (0, f)),
                  pl.BlockSpec((tf, d), lambda i, f: (f, 0)),
                  pl.BlockSpec((1, d), lambda i, f: (0, 0)),
                  pl.BlockSpec((1, d), lambda i, f: (0, 0))],
        out_specs=pl.BlockSpec((tm, d), lambda i, f: (i, 0)),
        out_shape=jax.ShapeDtypeStruct((s, d), F32),
        scratch_shapes=[pltpu.VMEM((tm, d), BF16)],
        compiler_params=_cparams(("parallel", "arbitrary")),
        name="ffn_ln",
    )(x, w_gate, w_up, w_down, ln_g, ln_b)


def _moba_perm():
    half = MOBA_ROT // 2
    return (list(range(0, half)) + list(range(MOBA_ROT, MOBA_ROT + 48))
            + list(range(half, MOBA_ROT)) + list(range(MOBA_ROT + 48, MOBA_DIM)))


def _relayout_w_in(w_in):
    d = w_in.shape[0]
    c_q, c_kv, k_rope, mq, mk, mv, conv_in, pool_in = jnp.split(
        w_in, [512, 768, 832, 1344, 1856, 2368, 3392], axis=1)
    perm = jnp.array(_moba_perm(), jnp.int32)
    permute = lambda w: w.reshape(d, MOBA_HEADS, MOBA_DIM)[:, :, perm].reshape(d, W_GROUP)
    z32 = jnp.zeros((d, 32), w_in.dtype)
    kr = jnp.concatenate([k_rope[:, :32], z32, k_rope[:, 32:], z32], axis=1)
    pad = jnp.zeros((d, IN_PAD - COL_KROPE - LANES), w_in.dtype)
    return jnp.concatenate([conv_in[:, :512], conv_in[:, 512:], c_q, pool_in,
                            permute(mq), permute(mk), mv, c_kv, kr, pad], axis=1)


def _relayout_w_uq(w_uq):
    r = w_uq.shape[0]
    w = w_uq.reshape(r, MLA_HEADS, MLA_QK)
    z32 = jnp.zeros((r, MLA_HEADS, 32), w_uq.dtype)
    half = MLA_ROPE // 2
    out = jnp.concatenate([w[:, :, :MLA_NOPE], w[:, :, MLA_NOPE:MLA_NOPE + half], z32,
                           w[:, :, MLA_NOPE + half:], z32], axis=2)
    return out.reshape(r, MLA_HEADS * QK_PAD)


def _rope_tables(positions):
    pos = positions.astype(F32)[0][:, None]

    def cos_sin(dim):
        inv_freq = 1.0 / (ROPE_THETA ** (jnp.arange(0, dim, 2, dtype=F32) / dim))
        ang = pos * inv_freq
        return jnp.cos(ang), jnp.sin(ang)

    s = pos.shape[0]
    c, sn = cos_sin(MLA_ROPE)
    z = jnp.zeros((s, 32), F32)
    ca = jnp.concatenate([c, z, c, z], axis=1)
    sa = jnp.concatenate([-sn, z, sn, z], axis=1)
    c, sn = cos_sin(MOBA_ROT)
    one = jnp.ones((s, 48), F32)
    z = jnp.zeros((s, 48), F32)
    cm = jnp.concatenate([c, one, c, one], axis=1)
    sm = jnp.concatenate([-sn, z, sn, z], axis=1)
    return ca, sa, cm, sm


def kernel(x, positions, w_in, g_q, w_uq, g_kv, w_ukv, conv_dw_w, conv_dw_b, conv_ln_g,
           conv_ln_b, conv_pw_w, pool_w, pool_scale, w_out, ln1_g, ln1_b, w_gate, w_up,
           w_down, ln2_g, ln2_b):
    b, s, d = x.shape
    assert b == 1 and d == D_MODEL and s % 1024 == 0
    ca, sa, cm, sm = _rope_tables(positions)
    row2 = lambda a: a.reshape(1, -1)
    h = x[0]
    for l in range(DEPTH):
        w_in_p = _relayout_w_in(w_in[l]).astype(BF16)
        w_uq_p = _relayout_w_uq(w_uq[l]).astype(BF16)
        w_att = w_out[l][:2 * W_GROUP].astype(BF16)
        w_mix = w_out[l][2 * W_GROUP:].astype(BF16)
        dw_w = jnp.pad(conv_dw_w[l], ((0, 1), (0, 0)))

        proj = _matmul(h, w_in_p, F32, tm=1024, tn=1024)
        q8, k8, v8 = _prep_attn(proj, row2(g_q[l]), row2(g_kv[l]), w_uq_p,
                                w_ukv[l].astype(BF16), ca, sa, cm, sm, tr=256)
        mix = _prep_mix(proj, dw_w, row2(conv_dw_b[l]), row2(conv_ln_g[l]),
                        row2(conv_ln_b[l]), conv_pw_w[l].astype(BF16),
                        pool_w[l].astype(BF16), row2(pool_scale[l]), tr=256)
        att = _attention(q8, k8, v8, tq=512, tk=512)
        x1 = _out_proj(att, mix, h, w_att, w_mix, row2(ln1_g[l]), row2(ln1_b[l]), tm=256)
        h = _ffn(x1, w_gate[l].astype(BF16), w_up[l].astype(BF16), w_down[l].astype(BF16),
                 row2(ln2_g[l]), row2(ln2_b[l]), tm=512, tf=512)
    return h[None]
```

```python
import functools

import jax
import jax.numpy as jnp
from jax import lax
from jax.experimental import pallas as pl
from jax.experimental.pallas import tpu as pltpu

F32 = jnp.float32
BF16 = jnp.bfloat16

D_MODEL = 2048
W_GROUP = 512
MLA_HEADS = 4
MLA_NOPE = 128
MLA_ROPE = 64
MLA_V = 128
MLA_QK = MLA_NOPE + MLA_ROPE
Q_LORA = 512
KV_LORA = 256
MOBA_HEADS = 4
MOBA_DIM = 128
MOBA_BLOCK = 256
MOBA_TOPK = 3
MOBA_ROT = 32
CONV_WIDTH = 31
POOL_WINDOWS = (2, 4, 8, 16)
POOL_GROUP = 128
ROPE_THETA = 500000.0
NORM_EPS = 1e-5
NEG_INF = -1e30
DEPTH = 2
DEEPNORM_ALPHA = (2 * DEPTH) ** 0.25

LANES = 128
HEADS = MLA_HEADS + MOBA_HEADS
QK_PAD = 256
LOG2E = 1.4426950408889634
VMEM_LIMIT = 56 * 1024 * 1024

COL_CONV_A = 0
COL_CONV_G = 512
COL_CQ = 1024
COL_POOL = 1536
COL_MQ = 2048
COL_MK = 2560
COL_MV = 3072
COL_CKV = 3584
COL_KROPE = 3840
IN_PAD = 4096


def _cparams(sem):
    return pltpu.CompilerParams(dimension_semantics=sem, vmem_limit_bytes=VMEM_LIMIT)


def _matmul_kernel(x_ref, w_ref, o_ref):
    o_ref[...] = jnp.dot(x_ref[...].astype(BF16), w_ref[...],
                         preferred_element_type=F32).astype(o_ref.dtype)


def _matmul(x, w, out_dtype, tm, tn):
    m, k = x.shape
    n = w.shape[1]
    return pl.pallas_call(
        _matmul_kernel,
        grid=(m // tm, n // tn),
        in_specs=[pl.BlockSpec((tm, k), lambda i, j: (i, 0)),
                  pl.BlockSpec((k, tn), lambda i, j: (0, j))],
        out_specs=pl.BlockSpec((tm, tn), lambda i, j: (i, j)),
        out_shape=jax.ShapeDtypeStruct((m, n), out_dtype),
        compiler_params=_cparams(("parallel", "parallel")),
        name="in_proj",
    )(x, w)


def _rms(x, g):
    return x * lax.rsqrt(jnp.mean(x * x, axis=-1, keepdims=True) + NORM_EPS) * g


def _rope128(r, c, s):
    return r * c + pltpu.roll(r, 64, 1) * s


def _prep_attn_kernel(cq_ref, mq_ref, mk_ref, mv_ref, ckv_ref, kr_ref,
                      gq_ref, gkv_ref, wuq_ref, wukv_ref,
                      ca_ref, sa_ref, cm_ref, sm_ref,
                      q8_ref, k8_ref, v8_ref, kmean_ref, *, tr):
    i = pl.program_id(0)

    @pl.when(i == 0)
    def _():
        kmean_ref[...] = jnp.zeros_like(kmean_ref)

    ca = ca_ref[...]
    sa = sa_ref[...]
    cm = cm_ref[...]
    sm = sm_ref[...]
    sc_a = MLA_QK ** -0.5 * LOG2E
    sc_b = MOBA_DIM ** -0.5 * LOG2E

    qn = _rms(cq_ref[...], gq_ref[...]).astype(BF16)
    q = jnp.dot(qn, wuq_ref[...], preferred_element_type=F32)
    for h in range(MLA_HEADS):
        base = h * QK_PAD
        q8_ref[h, :, 0:LANES] = (q[:, base:base + LANES] * sc_a).astype(BF16)
        pe = _rope128(q[:, base + LANES:base + 2 * LANES], ca, sa)
        q8_ref[h, :, LANES:2 * LANES] = (pe * sc_a).astype(BF16)

    kvn = _rms(ckv_ref[...], gkv_ref[...]).astype(BF16)
    kv = jnp.dot(kvn, wukv_ref[...], preferred_element_type=F32)
    kpe = _rope128(kr_ref[...], ca, sa).astype(BF16)
    for h in range(MLA_HEADS):
        base = h * 2 * LANES
        k8_ref[h, :, 0:LANES] = kv[:, base:base + LANES].astype(BF16)
        k8_ref[h, :, LANES:2 * LANES] = kpe
        v8_ref[h] = kv[:, base + LANES:base + 2 * LANES].astype(BF16)

    row = i * tr + lax.broadcasted_iota(jnp.int32, (tr, 1), 0)
    own = lax.shift_right_logical(row, 8)
    lane = lax.broadcasted_iota(jnp.int32, (tr, LANES), 1).astype(F32)
    own = own.astype(F32)
    valid = lane < own
    is_own = lane == own
    onehot = jnp.where(is_own, 1.0, 0.0).astype(BF16)
    blocks_per_tile = tr // MOBA_BLOCK
    for h in range(MOBA_HEADS):
        sl = slice(h * LANES, (h + 1) * LANES)
        kb = _rope128(mk_ref[:, sl], cm, sm)
        qb = _rope128(mq_ref[:, sl], cm, sm)
        for b in range(blocks_per_tile):
            ksum = jnp.sum(kb[b * MOBA_BLOCK:(b + 1) * MOBA_BLOCK], axis=0, keepdims=True)
            kmean_ref[h, pl.ds(i * blocks_per_tile + b, 1), :] = ksum * (1.0 / MOBA_BLOCK)
        gate = lax.dot_general(qb, kmean_ref[h], (((1,), (1,)), ((), ())),
                               precision=lax.Precision.HIGHEST,
                               preferred_element_type=F32)
        g = jnp.where(valid, gate, NEG_INF)
        sel = jnp.zeros((tr, LANES), jnp.bool_)
        for _ in range(MOBA_TOPK):
            mx = jnp.max(g, axis=1, keepdims=True)
            idx = jnp.min(jnp.where(g == mx, lane, float(LANES)), axis=1, keepdims=True)
            hit = lane == idx
            sel = jnp.logical_or(sel, hit)
            g = jnp.where(hit, -3.0e38, g)
        ok = jnp.logical_or(jnp.logical_and(sel, valid), is_own)
        bias = jnp.where(ok, 0.0, NEG_INF)
        q8_ref[MLA_HEADS + h, :, 0:LANES] = (qb * sc_b).astype(BF16)
        q8_ref[MLA_HEADS + h, :, LANES:2 * LANES] = bias.astype(BF16)
        k8_ref[MLA_HEADS + h, :, 0:LANES] = kb.astype(BF16)
        k8_ref[MLA_HEADS + h, :, LANES:2 * LANES] = onehot
        v8_ref[MLA_HEADS + h] = mv_ref[:, sl].astype(BF16)


def _prep_attn(h, g_q, g_kv, w_uq_p, w_ukv, ca, sa, cm, sm, tr):
    s = h.shape[0]
    hb = lambda w, c: pl.BlockSpec((tr, w), lambda i, c=c: (i, c))
    full = lambda a: pl.BlockSpec(a.shape, lambda i: (0,) * a.ndim)
    tab = pl.BlockSpec((tr, LANES), lambda i: (i, 0))
    out3 = lambda w: pl.BlockSpec((HEADS, tr, w), lambda i: (0, i, 0))
    return pl.pallas_call(
        functools.partial(_prep_attn_kernel, tr=tr),
        grid=(s // tr,),
        in_specs=[hb(512, COL_CQ // 512), hb(512, COL_MQ // 512), hb(512, COL_MK // 512),
                  hb(512, COL_MV // 512), hb(256, COL_CKV // 256), hb(128, COL_KROPE // 128),
                  full(g_q), full(g_kv), full(w_uq_p), full(w_ukv), tab, tab, tab, tab],
        out_specs=[out3(QK_PAD), out3(QK_PAD), out3(LANES)],
        out_shape=[jax.ShapeDtypeStruct((HEADS, s, QK_PAD), BF16),
                   jax.ShapeDtypeStruct((HEADS, s, QK_PAD), BF16),
                   jax.ShapeDtypeStruct((HEADS, s, LANES), BF16)],
        scratch_shapes=[pltpu.VMEM((MOBA_HEADS, LANES, LANES), F32)],
        compiler_params=_cparams(("arbitrary",)),
        name="prep_attn",
    )(h, h, h, h, h, h, g_q, g_kv, w_uq_p, w_ukv, ca, sa, cm, sm)


CONV_HALO = 32
POOL_HALO = 16


def _prep_mix_kernel(a_ref, g_ref, p_ref, dww_ref, dwb_ref, lng_ref, lnb_ref, pww_ref,
                     poolw_ref, pscale_ref, o_ref, hbuf, pbuf, *, tr):
    i = pl.program_id(0)

    @pl.when(i == 0)
    def _():
        hbuf[0:CONV_HALO, :] = jnp.zeros((CONV_HALO, W_GROUP), F32)
        pbuf[0:POOL_HALO, :] = jnp.zeros((POOL_HALO, W_GROUP), F32)

    @pl.when(i > 0)
    def _():
        hbuf[0:CONV_HALO, :] = hbuf[tr:tr + CONV_HALO, :]
        pbuf[0:POOL_HALO, :] = pbuf[tr:tr + POOL_HALO, :]

    hdn = a_ref[...] * jax.nn.sigmoid(g_ref[...])
    hbuf[CONV_HALO:CONV_HALO + tr, :] = hdn
    y = jnp.zeros((tr, W_GROUP), F32) + dwb_ref[...]
    first = CONV_HALO - (CONV_WIDTH - 1)
    for j in range(CONV_WIDTH):
        y = y + dww_ref[j:j + 1, :] * hbuf[first + j:first + j + tr, :]
    mu = jnp.mean(y, axis=-1, keepdims=True)
    yc = y - mu
    var = jnp.mean(yc * yc, axis=-1, keepdims=True)
    z = yc * lax.rsqrt(var + NORM_EPS) * lng_ref[...] + lnb_ref[...]
    z = z * jax.nn.sigmoid(z)
    o_ref[:, 0:W_GROUP] = jnp.dot(z.astype(BF16), pww_ref[...],
                                  preferred_element_type=F32).astype(o_ref.dtype)

    u = p_ref[...]
    pbuf[POOL_HALO:POOL_HALO + tr, :] = u
    t = i * tr + lax.broadcasted_iota(jnp.int32, (tr, 1), 0)
    for gi, w in enumerate(POOL_WINDOWS):
        sl = slice(gi * POOL_GROUP, (gi + 1) * POOL_GROUP)
        ssum = u[:, sl]
        for d in range(1, w):
            ssum = ssum + pbuf[POOL_HALO - d:POOL_HALO - d + tr, sl]
        count = jnp.minimum(t + 1, w).astype(F32)
        pooled = ssum / count - u[:, sl]
        mixed = jnp.dot(pooled.astype(BF16), poolw_ref[gi], preferred_element_type=F32)
        o_ref[:, W_GROUP + gi * POOL_GROUP:W_GROUP + (gi + 1) * POOL_GROUP] = (
            mixed * pscale_ref[:, sl]).astype(o_ref.dtype)


def _prep_mix(h, dw_w, dw_b, ln_g, ln_b, pw_w, pool_w, pool_scale, tr):
    s = h.shape[0]
    hb = lambda c: pl.BlockSpec((tr, 512), lambda i, c=c: (i, c))
    full = lambda a: pl.BlockSpec(a.shape, lambda i: (0,) * a.ndim)
    return pl.pallas_call(
        functools.partial(_prep_mix_kernel, tr=tr),
        grid=(s // tr,),
        in_specs=[hb(COL_CONV_A // 512), hb(COL_CONV_G // 512), hb(COL_POOL // 512),
                  full(dw_w), full(dw_b), full(ln_g), full(ln_b), full(pw_w),
                  full(pool_w), full(pool_scale)],
        out_specs=pl.BlockSpec((tr, 2 * W_GROUP), lambda i: (i, 0)),
        out_shape=jax.ShapeDtypeStruct((s, 2 * W_GROUP), BF16),
        scratch_shapes=[pltpu.VMEM((CONV_HALO + tr, W_GROUP), F32),
                        pltpu.VMEM((POOL_HALO + tr, W_GROUP), F32)],
        compiler_params=_cparams(("arbitrary",)),
        name="prep_mix",
    )(h, h, h, dw_w, dw_b, ln_g, ln_b, pw_w, pool_w, pool_scale)


def _attn_kernel(q_ref, k_ref, v_ref, o_ref, s_buf, p_buf, m_ref, l_ref, acc_ref, *, t, nh):
    i = pl.program_id(1)
    n = i + 1

    def scores(hd, j, slot):
        k = k_ref[hd, pl.ds(pl.multiple_of(j * t, t), t), :]
        s_buf[slot, hd] = lax.dot_general(q_ref[hd], k, (((1,), (1,)), ((), ())),
                                          preferred_element_type=F32)

    def softmax(hd, slot, diagonal):
        cols = [s_buf[slot, hd, :, c * LANES:(c + 1) * LANES] for c in range(t // LANES)]
        if diagonal:
            rq = lax.broadcasted_iota(jnp.int32, (t, LANES), 0)
            ck = lax.broadcasted_iota(jnp.int32, (t, LANES), 1)
            cols = [jnp.where(ck + c * LANES <= rq, s, NEG_INF) for c, s in enumerate(cols)]
        mx = functools.reduce(jnp.maximum, cols)
        m_old = m_ref[hd]
        m_new = jnp.maximum(m_old, jnp.max(mx, axis=1, keepdims=True))
        alpha = jnp.exp2(m_old - m_new)
        ps = [jnp.exp2(s - m_new) for s in cols]
        row = functools.reduce(lambda a, b: a + b, ps)
        l_ref[hd] = alpha * l_ref[hd] + jnp.sum(row, axis=1, keepdims=True)
        m_ref[hd] = m_new
        for c, p in enumerate(ps):
            p_buf[slot, hd, :, c * LANES:(c + 1) * LANES] = p.astype(BF16)
        return alpha

    def values(hd, j, slot):
        v = v_ref[hd, pl.ds(pl.multiple_of(j * t, t), t), :]
        return jnp.dot(p_buf[slot, hd], v, preferred_element_type=F32)

    def step(j, slot, *, ahead, behind, diagonal):
        for hd in range(nh):
            pv = values(hd, j - 1, 1 - slot) if behind else None
            if ahead:
                scores(hd, j + 1, 1 - slot)
            alpha = softmax(hd, slot, diagonal)
            if behind:
                acc_ref[hd] = alpha * (acc_ref[hd] + pv)

    def finish(j, slot):
        for hd in range(nh):
            acc = acc_ref[hd] + values(hd, j, slot)
            o_ref[:, hd * LANES:(hd + 1) * LANES] = (acc / l_ref[hd]).astype(o_ref.dtype)

    m_ref[...] = jnp.full(m_ref.shape, NEG_INF, F32)
    l_ref[...] = jnp.zeros(l_ref.shape, F32)
    acc_ref[...] = jnp.zeros(acc_ref.shape, F32)
    for hd in range(nh):
        scores(hd, 0, 0)

    @pl.when(n == 1)
    def _():
        step(0, 0, ahead=False, behind=False, diagonal=True)
        finish(0, 0)

    @pl.when(n > 1)
    def _():
        step(0, 0, ahead=True, behind=False, diagonal=False)

        def body(u, carry):
            j = 2 * u + 1
            step(j, 1, ahead=True, behind=True, diagonal=False)
            step(j + 1, 0, ahead=True, behind=True, diagonal=False)
            return carry

        lax.fori_loop(0, lax.div(n - 2, 2), body, 0)

        @pl.when(lax.rem(n, 2) == 1)
        def _():
            step(n - 2, 1, ahead=True, behind=True, diagonal=False)
            step(n - 1, 0, ahead=False, behind=True, diagonal=True)
            finish(n - 1, 0)

        @pl.when(lax.rem(n, 2) == 0)
        def _():
            step(n - 1, 1, ahead=False, behind=True, diagonal=True)
            finish(n - 1, 1)


def _attention(q8, k8, v8, t, nh):
    heads, s, _ = q8.shape
    resident = lambda w: pl.BlockSpec((nh, s, w), lambda h, i: (h, 0, 0),
                                      pipeline_mode=pl.Buffered(1))
    return pl.pallas_call(
        functools.partial(_attn_kernel, t=t, nh=nh),
        grid=(heads // nh, s // t),
        in_specs=[pl.BlockSpec((nh, t, QK_PAD), lambda h, i: (h, i, 0)),
                  resident(QK_PAD), resident(LANES)],
        out_specs=pl.BlockSpec((t, nh * LANES), lambda h, i: (i, h)),
        out_shape=jax.ShapeDtypeStruct((s, heads * LANES), BF16),
        scratch_shapes=[pltpu.VMEM((2, nh, t, t), F32), pltpu.VMEM((2, nh, t, t), BF16),
                        pltpu.VMEM((nh, t, LANES), F32), pltpu.VMEM((nh, t, LANES), F32),
                        pltpu.VMEM((nh, t, LANES), F32)],
        compiler_params=_cparams(("parallel", "arbitrary")),
        name="attention",
    )(q8, k8, v8)


def _layer_norm(y, g, b):
    mu = jnp.mean(y, axis=-1, keepdims=True)
    yc = y - mu
    var = jnp.mean(yc * yc, axis=-1, keepdims=True)
    return yc * lax.rsqrt(var + NORM_EPS) * g + b


def _out_proj_kernel(att_ref, mix_ref, x_ref, wa_ref, wm_ref, g_ref, b_ref, o_ref):
    y = jnp.dot(att_ref[...], wa_ref[...], preferred_element_type=F32)
    y = y + jnp.dot(mix_ref[...], wm_ref[...], preferred_element_type=F32)
    y = y + DEEPNORM_ALPHA * x_ref[...]
    o_ref[...] = _layer_norm(y, g_ref[...], b_ref[...])


def _out_proj(att, mix, x, w_att, w_mix, ln_g, ln_b, tm):
    s, d = x.shape
    row = lambda w: pl.BlockSpec((tm, w), lambda i: (i, 0))
    full = lambda a: pl.BlockSpec(a.shape, lambda i: (0,) * a.ndim)
    return pl.pallas_call(
        _out_proj_kernel,
        grid=(s // tm,),
        in_specs=[row(att.shape[1]), row(mix.shape[1]), row(d), full(w_att), full(w_mix),
                  full(ln_g), full(ln_b)],
        out_specs=row(d),
        out_shape=jax.ShapeDtypeStruct((s, d), F32),
        compiler_params=_cparams(("parallel",)),
        name="out_proj_ln",
    )(att, mix, x, w_att, w_mix, ln_g, ln_b)


def _ffn_kernel(x_ref, wg_ref, wu_ref, wd_ref, g_ref, b_ref, o_ref, xb_ref):
    f = pl.program_id(1)

    @pl.when(f == 0)
    def _():
        x = x_ref[...]
        xb_ref[...] = x.astype(BF16)
        o_ref[...] = DEEPNORM_ALPHA * x

    xb = xb_ref[...]
    gate = jnp.dot(xb, wg_ref[...], preferred_element_type=F32)
    up = jnp.dot(xb, wu_ref[...], preferred_element_type=F32)
    act = (gate * jax.nn.sigmoid(gate) * up).astype(BF16)
    o_ref[...] += jnp.dot(act, wd_ref[...], preferred_element_type=F32)

    @pl.when(f == pl.num_programs(1) - 1)
    def _():
        o_ref[...] = _layer_norm(o_ref[...], g_ref[...], b_ref[...])


def _ffn(x, w_gate, w_up, w_down, ln_g, ln_b, tm, tf):
    s, d = x.shape
    dff = w_gate.shape[1]
    return pl.pallas_call(
        _ffn_kernel,
        grid=(s // tm, dff // tf),
        in_specs=[pl.BlockSpec((tm, d), lambda i, f: (i, 0)),
                  pl.BlockSpec((d, tf), lambda i, f: (0, f)),
                  pl.BlockSpec((d, tf), lambda i, f: (0, f)),
                  pl.BlockSpec((tf, d), lambda i, f: (f, 0)),
                  pl.BlockSpec((1, d), lambda i, f: (0, 0)),
                  pl.BlockSpec((1, d), lambda i, f: (0, 0))],
        out_specs=pl.BlockSpec((tm, d), lambda i, f: (i, 0)),
        out_shape=jax.ShapeDtypeStruct((s, d), F32),
        scratch_shapes=[pltpu.VMEM((tm, d), BF16)],
        compiler_params=_cparams(("parallel", "arbitrary")),
        name="ffn_ln",
    )(x, w_gate, w_up, w_down, ln_g, ln_b)


def _moba_perm():
    half = MOBA_ROT // 2
    return (list(range(0, half)) + list(range(MOBA_ROT, MOBA_ROT + 48))
            + list(range(half, MOBA_ROT)) + list(range(MOBA_ROT + 48, MOBA_DIM)))


def _relayout_w_in(w_in):
    d = w_in.shape[0]
    c_q, c_kv, k_rope, mq, mk, mv, conv_in, pool_in = jnp.split(
        w_in, [512, 768, 832, 1344, 1856, 2368, 3392], axis=1)
    perm = jnp.array(_moba_perm(), jnp.int32)
    permute = lambda w: w.reshape(d, MOBA_HEADS, MOBA_DIM)[:, :, perm].reshape(d, W_GROUP)
    z32 = jnp.zeros((d, 32), w_in.dtype)
    kr = jnp.concatenate([k_rope[:, :32], z32, k_rope[:, 32:], z32], axis=1)
    pad = jnp.zeros((d, IN_PAD - COL_KROPE - LANES), w_in.dtype)
    return jnp.concatenate([conv_in[:, :512], conv_in[:, 512:], c_q, pool_in,
                            permute(mq), permute(mk), mv, c_kv, kr, pad], axis=1)


def _relayout_w_uq(w_uq):
    r = w_uq.shape[0]
    w = w_uq.reshape(r, MLA_HEADS, MLA_QK)
    z32 = jnp.zeros((r, MLA_HEADS, 32), w_uq.dtype)
    half = MLA_ROPE // 2
    out = jnp.concatenate([w[:, :, :MLA_NOPE], w[:, :, MLA_NOPE:MLA_NOPE + half], z32,
                           w[:, :, MLA_NOPE + half:], z32], axis=2)
    return out.reshape(r, MLA_HEADS * QK_PAD)


def _rope_tables(positions):
    pos = positions.astype(F32)[0][:, None]

    def cos_sin(dim):
        inv_freq = 1.0 / (ROPE_THETA ** (jnp.arange(0, dim, 2, dtype=F32) / dim))
        ang = pos * inv_freq
        return jnp.cos(ang), jnp.sin(ang)

    s = pos.shape[0]
    c, sn = cos_sin(MLA_ROPE)
    z = jnp.zeros((s, 32), F32)
    ca = jnp.concatenate([c, z, c, z], axis=1)
    sa = jnp.concatenate([-sn, z, sn, z], axis=1)
    c, sn = cos_sin(MOBA_ROT)
    one = jnp.ones((s, 48), F32)
    z = jnp.zeros((s, 48), F32)
    cm = jnp.concatenate([c, one, c, one], axis=1)
    sm = jnp.concatenate([-sn, z, sn, z], axis=1)
    return ca, sa, cm, sm


def kernel(x, positions, w_in, g_q, w_uq, g_kv, w_ukv, conv_dw_w, conv_dw_b, conv_ln_g,
           conv_ln_b, conv_pw_w, pool_w, pool_scale, w_out, ln1_g, ln1_b, w_gate, w_up,
           w_down, ln2_g, ln2_b):
    b, s, d = x.shape
    assert b == 1 and d == D_MODEL and s % 1024 == 0
    ca, sa, cm, sm = _rope_tables(positions)
    row2 = lambda a: a.reshape(1, -1)
    h = x[0]
    for l in range(DEPTH):
        w_in_p = _relayout_w_in(w_in[l]).astype(BF16)
        w_uq_p = _relayout_w_uq(w_uq[l]).astype(BF16)
        w_att = w_out[l][:2 * W_GROUP].astype(BF16)
        w_mix = w_out[l][2 * W_GROUP:].astype(BF16)
        dw_w = jnp.pad(conv_dw_w[l], ((0, 1), (0, 0)))

        proj = _matmul(h, w_in_p, F32, tm=1024, tn=1024)
        q8, k8, v8 = _prep_attn(proj, row2(g_q[l]), row2(g_kv[l]), w_uq_p,
                                w_ukv[l].astype(BF16), ca, sa, cm, sm, tr=256)
        mix = _prep_mix(proj, dw_w, row2(conv_dw_b[l]), row2(conv_ln_g[l]),
                        row2(conv_ln_b[l]), conv_pw_w[l].astype(BF16),
                        pool_w[l].astype(BF16), row2(pool_scale[l]), tr=256)
        att = _attention(q8, k8, v8, t=512, nh=2)
        x1 = _out_proj(att, mix, h, w_att, w_mix, row2(ln1_g[l]), row2(ln1_b[l]), tm=256)
        h = _ffn(x1, w_gate[l].astype(BF16), w_up[l].astype(BF16), w_down[l].astype(BF16),
                 row2(ln2_g[l]), row2(ln2_b[l]), tm=512, tf=512)
    return h[None]
```

```python
import functools

import jax
import jax.numpy as jnp
from jax import lax
from jax.experimental import pallas as pl
from jax.experimental.pallas import tpu as pltpu

F32 = jnp.float32
BF16 = jnp.bfloat16

D_MODEL = 2048
W_GROUP = 512
MLA_HEADS = 4
MLA_NOPE = 128
MLA_ROPE = 64
MLA_V = 128
MLA_QK = MLA_NOPE + MLA_ROPE
Q_LORA = 512
KV_LORA = 256
MOBA_HEADS = 4
MOBA_DIM = 128
MOBA_BLOCK = 256
MOBA_TOPK = 3
MOBA_ROT = 32
CONV_WIDTH = 31
POOL_WINDOWS = (2, 4, 8, 16)
POOL_GROUP = 128
ROPE_THETA = 500000.0
NORM_EPS = 1e-5
NEG_INF = -1e30
DEPTH = 2
DEEPNORM_ALPHA = (2 * DEPTH) ** 0.25

LANES = 128
SUBLANES = 8
HEADS = MLA_HEADS + MOBA_HEADS
QK_PAD = 256
LOG2E = 1.4426950408889634
SOFTMAX_ROWS = 64
VMEM_LIMIT = 56 * 1024 * 1024

COL_CONV_A = 0
COL_CONV_G = 512
COL_CQ = 1024
COL_POOL = 1536
COL_MQ = 2048
COL_MK = 2560
COL_MV = 3072
COL_CKV = 3584
COL_KROPE = 3840
IN_PAD = 4096


def _cparams(sem):
    return pltpu.CompilerParams(dimension_semantics=sem, vmem_limit_bytes=VMEM_LIMIT)


def _matmul_kernel(x_ref, w_ref, o_ref):
    o_ref[...] = jnp.dot(x_ref[...].astype(BF16), w_ref[...],
                         preferred_element_type=F32).astype(o_ref.dtype)


def _matmul(x, w, out_dtype, tm, tn):
    m, k = x.shape
    n = w.shape[1]
    return pl.pallas_call(
        _matmul_kernel,
        grid=(m // tm, n // tn),
        in_specs=[pl.BlockSpec((tm, k), lambda i, j: (i, 0)),
                  pl.BlockSpec((k, tn), lambda i, j: (0, j))],
        out_specs=pl.BlockSpec((tm, tn), lambda i, j: (i, j)),
        out_shape=jax.ShapeDtypeStruct((m, n), out_dtype),
        compiler_params=_cparams(("parallel", "parallel")),
        name="in_proj",
    )(x, w)


def _rms(x, g):
    return x * lax.rsqrt(jnp.mean(x * x, axis=-1, keepdims=True) + NORM_EPS) * g


def _rope128(r, c, s):
    return r * c + pltpu.roll(r, 64, 1) * s


def _prep_attn_kernel(cq_ref, mq_ref, mk_ref, mv_ref, ckv_ref, kr_ref,
                      gq_ref, gkv_ref, wuq_ref, wukv_ref,
                      ca_ref, sa_ref, cm_ref, sm_ref,
                      q8_ref, k8_ref, v8_ref, kmean_ref, *, tr):
    i = pl.program_id(0)

    @pl.when(i == 0)
    def _():
        kmean_ref[...] = jnp.zeros_like(kmean_ref)

    ca = ca_ref[...]
    sa = sa_ref[...]
    cm = cm_ref[...]
    sm = sm_ref[...]
    sc_a = MLA_QK ** -0.5 * LOG2E
    sc_b = MOBA_DIM ** -0.5 * LOG2E

    qn = _rms(cq_ref[...], gq_ref[...]).astype(BF16)
    q = jnp.dot(qn, wuq_ref[...], preferred_element_type=F32)
    for h in range(MLA_HEADS):
        base = h * QK_PAD
        q8_ref[h, :, 0:LANES] = (q[:, base:base + LANES] * sc_a).astype(BF16)
        pe = _rope128(q[:, base + LANES:base + 2 * LANES], ca, sa)
        q8_ref[h, :, LANES:2 * LANES] = (pe * sc_a).astype(BF16)

    kvn = _rms(ckv_ref[...], gkv_ref[...]).astype(BF16)
    kv = jnp.dot(kvn, wukv_ref[...], preferred_element_type=F32)
    kpe = _rope128(kr_ref[...], ca, sa).astype(BF16)
    for h in range(MLA_HEADS):
        base = h * 2 * LANES
        k8_ref[h, :, 0:LANES] = kv[:, base:base + LANES].astype(BF16)
        k8_ref[h, :, LANES:2 * LANES] = kpe
        v8_ref[h] = kv[:, base + LANES:base + 2 * LANES].astype(BF16)

    row = i * tr + lax.broadcasted_iota(jnp.int32, (tr, 1), 0)
    own = lax.shift_right_logical(row, 8)
    lane = lax.broadcasted_iota(jnp.int32, (tr, LANES), 1).astype(F32)
    own = own.astype(F32)
    valid = lane < own
    is_own = lane == own
    onehot = jnp.where(is_own, 1.0, 0.0).astype(BF16)
    blocks_per_tile = tr // MOBA_BLOCK
    for h in range(MOBA_HEADS):
        sl = slice(h * LANES, (h + 1) * LANES)
        kb = _rope128(mk_ref[:, sl], cm, sm)
        qb = _rope128(mq_ref[:, sl], cm, sm)
        for b in range(blocks_per_tile):
            ksum = jnp.sum(kb[b * MOBA_BLOCK:(b + 1) * MOBA_BLOCK], axis=0, keepdims=True)
            kmean_ref[h, pl.ds(i * blocks_per_tile + b, 1), :] = ksum * (1.0 / MOBA_BLOCK)
        gate = lax.dot_general(qb, kmean_ref[h], (((1,), (1,)), ((), ())),
                               precision=lax.Precision.HIGHEST,
                               preferred_element_type=F32)
        g = jnp.where(valid, gate, NEG_INF)
        sel = jnp.zeros((tr, LANES), jnp.bool_)
        for _ in range(MOBA_TOPK):
            mx = jnp.max(g, axis=1, keepdims=True)
            idx = jnp.min(jnp.where(g == mx, lane, float(LANES)), axis=1, keepdims=True)
            hit = lane == idx
            sel = jnp.logical_or(sel, hit)
            g = jnp.where(hit, -3.0e38, g)
        ok = jnp.logical_or(jnp.logical_and(sel, valid), is_own)
        bias = jnp.where(ok, 0.0, NEG_INF)
        q8_ref[MLA_HEADS + h, :, 0:LANES] = (qb * sc_b).astype(BF16)
        q8_ref[MLA_HEADS + h, :, LANES:2 * LANES] = bias.astype(BF16)
        k8_ref[MLA_HEADS + h, :, 0:LANES] = kb.astype(BF16)
        k8_ref[MLA_HEADS + h, :, LANES:2 * LANES] = onehot
        v8_ref[MLA_HEADS + h] = mv_ref[:, sl].astype(BF16)


def _prep_attn(h, g_q, g_kv, w_uq_p, w_ukv, ca, sa, cm, sm, tr):
    s = h.shape[0]
    hb = lambda w, c: pl.BlockSpec((tr, w), lambda i, c=c: (i, c))
    full = lambda a: pl.BlockSpec(a.shape, lambda i: (0,) * a.ndim)
    tab = pl.BlockSpec((tr, LANES), lambda i: (i, 0))
    out3 = lambda w: pl.BlockSpec((HEADS, tr, w), lambda i: (0, i, 0))
    return pl.pallas_call(
        functools.partial(_prep_attn_kernel, tr=tr),
        grid=(s // tr,),
        in_specs=[hb(512, COL_CQ // 512), hb(512, COL_MQ // 512), hb(512, COL_MK // 512),
                  hb(512, COL_MV // 512), hb(256, COL_CKV // 256), hb(128, COL_KROPE // 128),
                  full(g_q), full(g_kv), full(w_uq_p), full(w_ukv), tab, tab, tab, tab],
        out_specs=[out3(QK_PAD), out3(QK_PAD), out3(LANES)],
        out_shape=[jax.ShapeDtypeStruct((HEADS, s, QK_PAD), BF16),
                   jax.ShapeDtypeStruct((HEADS, s, QK_PAD), BF16),
                   jax.ShapeDtypeStruct((HEADS, s, LANES), BF16)],
        scratch_shapes=[pltpu.VMEM((MOBA_HEADS, LANES, LANES), F32)],
        compiler_params=_cparams(("arbitrary",)),
        name="prep_attn",
    )(h, h, h, h, h, h, g_q, g_kv, w_uq_p, w_ukv, ca, sa, cm, sm)


CONV_ROWS = 64
CONV_HALO = 32
POOL_HALO = 16


def _prep_mix_kernel(a_ref, g_ref, p_ref, dww_ref, dwb_ref, lng_ref, lnb_ref, pww_ref,
                     poolw_ref, pscale_ref, o_ref, hbuf, pbuf, shift_buf, y_buf, *, tr):
    i = pl.program_id(0)

    @pl.when(i == 0)
    def _():
        hbuf[0:CONV_HALO, :] = jnp.zeros((CONV_HALO, W_GROUP), F32)
        pbuf[0:POOL_HALO, :] = jnp.zeros((POOL_HALO, W_GROUP), F32)

    @pl.when(i > 0)
    def _():
        hbuf[0:CONV_HALO, :] = hbuf[tr:tr + CONV_HALO, :]
        pbuf[0:POOL_HALO, :] = pbuf[tr:tr + POOL_HALO, :]

    hdn = a_ref[...] * jax.nn.sigmoid(g_ref[...])
    hbuf[CONV_HALO:CONV_HALO + tr, :] = hdn
    first = CONV_HALO - (CONV_WIDTH - 1)
    for c in range(W_GROUP // LANES):
        cs = slice(c * LANES, (c + 1) * LANES)
        shifted = shift_buf.at[c % 2]
        for b in range(SUBLANES):
            span = tr + CONV_WIDTH - 1 - b - (CONV_WIDTH - 1 - b) % SUBLANES
            shifted[b, 0:span, :] = hbuf[first + b:first + b + span, cs]
        for r in range(0, tr, CONV_ROWS):
            acc = jnp.zeros((CONV_ROWS, LANES), F32) + dwb_ref[:, cs]
            for j in range(CONV_WIDTH):
                b = j % SUBLANES
                acc = acc + dww_ref[j:j + 1, cs] * shifted[b, r + j - b:r + j - b + CONV_ROWS, :]
            y_buf[r:r + CONV_ROWS, cs] = acc
    y = y_buf[...]
    mu = jnp.mean(y, axis=-1, keepdims=True)
    yc = y - mu
    var = jnp.mean(yc * yc, axis=-1, keepdims=True)
    z = yc * lax.rsqrt(var + NORM_EPS) * lng_ref[...] + lnb_ref[...]
    z = z * jax.nn.sigmoid(z)
    o_ref[:, 0:W_GROUP] = jnp.dot(z.astype(BF16), pww_ref[...],
                                  preferred_element_type=F32).astype(o_ref.dtype)

    u = p_ref[...]
    pbuf[POOL_HALO:POOL_HALO + tr, :] = u
    t = i * tr + lax.broadcasted_iota(jnp.int32, (tr, 1), 0)
    for gi, w in enumerate(POOL_WINDOWS):
        sl = slice(gi * POOL_GROUP, (gi + 1) * POOL_GROUP)
        ssum = u[:, sl]
        for d in range(1, w):
            ssum = ssum + pbuf[POOL_HALO - d:POOL_HALO - d + tr, sl]
        count = jnp.minimum(t + 1, w).astype(F32)
        pooled = ssum / count - u[:, sl]
        mixed = jnp.dot(pooled.astype(BF16), poolw_ref[gi], preferred_element_type=F32)
        o_ref[:, W_GROUP + gi * POOL_GROUP:W_GROUP + (gi + 1) * POOL_GROUP] = (
            mixed * pscale_ref[:, sl]).astype(o_ref.dtype)


def _prep_mix(h, dw_w, dw_b, ln_g, ln_b, pw_w, pool_w, pool_scale, tr):
    s = h.shape[0]
    hb = lambda c: pl.BlockSpec((tr, 512), lambda i, c=c: (i, c))
    full = lambda a: pl.BlockSpec(a.shape, lambda i: (0,) * a.ndim)
    return pl.pallas_call(
        functools.partial(_prep_mix_kernel, tr=tr),
        grid=(s // tr,),
        in_specs=[hb(COL_CONV_A // 512), hb(COL_CONV_G // 512), hb(COL_POOL // 512),
                  full(dw_w), full(dw_b), full(ln_g), full(ln_b), full(pw_w),
                  full(pool_w), full(pool_scale)],
        out_specs=pl.BlockSpec((tr, 2 * W_GROUP), lambda i: (i, 0)),
        out_shape=jax.ShapeDtypeStruct((s, 2 * W_GROUP), BF16),
        scratch_shapes=[pltpu.VMEM((CONV_HALO + tr, W_GROUP), F32),
                        pltpu.VMEM((POOL_HALO + tr, W_GROUP), F32),
                        pltpu.VMEM((2, SUBLANES, CONV_HALO + tr, LANES), F32),
                        pltpu.VMEM((tr, W_GROUP), F32)],
        compiler_params=_cparams(("arbitrary",)),
        name="prep_mix",
    )(h, h, h, dw_w, dw_b, ln_g, ln_b, pw_w, pool_w, pool_scale)


def _attn_kernel(q_ref, k_ref, v_ref, o_ref, s_buf, p_buf, m_ref, l_ref, acc_ref, *, t, nh):
    i = pl.program_id(1)
    n = i + 1

    def scores(hd, j, slot):
        k = k_ref[hd, pl.ds(pl.multiple_of(j * t, t), t), :]
        s_buf[slot, hd] = lax.dot_general(q_ref[hd], k, (((1,), (1,)), ((), ())),
                                          preferred_element_type=F32)

    def softmax(hd, slot, diagonal, pv):
        for r in range(t // SOFTMAX_ROWS):
            rows = slice(r * SOFTMAX_ROWS, (r + 1) * SOFTMAX_ROWS)
            cols = [s_buf[slot, hd, rows, c * LANES:(c + 1) * LANES] for c in range(t // LANES)]
            if diagonal:
                rq = r * SOFTMAX_ROWS + lax.broadcasted_iota(jnp.int32, (SOFTMAX_ROWS, LANES), 0)
                ck = lax.broadcasted_iota(jnp.int32, (SOFTMAX_ROWS, LANES), 1)
                cols = [jnp.where(ck + c * LANES <= rq, s, NEG_INF) for c, s in enumerate(cols)]
            mx = functools.reduce(jnp.maximum, cols)
            m_old = m_ref[hd, rows]
            m_new = jnp.maximum(m_old, jnp.max(mx, axis=1, keepdims=True))
            alpha = jnp.exp2(m_old - m_new)
            ps = [jnp.exp2(s - m_new) for s in cols]
            row = functools.reduce(lambda a, b: a + b, ps)
            l_ref[hd, rows] = alpha * l_ref[hd, rows] + jnp.sum(row, axis=1, keepdims=True)
            m_ref[hd, rows] = m_new
            for c, p in enumerate(ps):
                p_buf[slot, hd, rows, c * LANES:(c + 1) * LANES] = p.astype(BF16)
            if pv is not None:
                acc_ref[hd, rows] = alpha * (acc_ref[hd, rows] + pv[rows])

    def values(hd, j, slot):
        v = v_ref[hd, pl.ds(pl.multiple_of(j * t, t), t), :]
        return jnp.dot(p_buf[slot, hd], v, preferred_element_type=F32)

    def step(j, slot, *, ahead, behind, diagonal):
        for hd in range(nh):
            pv = values(hd, j - 1, 1 - slot) if behind else None
            if ahead:
                scores(hd, j + 1, 1 - slot)
            softmax(hd, slot, diagonal, pv)

    def finish(j, slot):
        for hd in range(nh):
            acc = acc_ref[hd] + values(hd, j, slot)
            o_ref[:, hd * LANES:(hd + 1) * LANES] = (acc / l_ref[hd]).astype(o_ref.dtype)

    m_ref[...] = jnp.full(m_ref.shape, NEG_INF, F32)
    l_ref[...] = jnp.zeros(l_ref.shape, F32)
    acc_ref[...] = jnp.zeros(acc_ref.shape, F32)
    for hd in range(nh):
        scores(hd, 0, 0)

    @pl.when(n == 1)
    def _():
        step(0, 0, ahead=False, behind=False, diagonal=True)
        finish(0, 0)

    @pl.when(n > 1)
    def _():
        step(0, 0, ahead=True, behind=False, diagonal=False)

        def body(u, carry):
            j = 2 * u + 1
            step(j, 1, ahead=True, behind=True, diagonal=False)
            step(j + 1, 0, ahead=True, behind=True, diagonal=False)
            return carry

        lax.fori_loop(0, lax.div(n - 2, 2), body, 0)

        @pl.when(lax.rem(n, 2) == 1)
        def _():
            step(n - 2, 1, ahead=True, behind=True, diagonal=False)
            step(n - 1, 0, ahead=False, behind=True, diagonal=True)
            finish(n - 1, 0)

        @pl.when(lax.rem(n, 2) == 0)
        def _():
            step(n - 1, 1, ahead=False, behind=True, diagonal=True)
            finish(n - 1, 1)


def _attention(q8, k8, v8, t, nh):
    heads, s, _ = q8.shape
    resident = lambda w: pl.BlockSpec((nh, s, w), lambda h, i: (h, 0, 0),
                                      pipeline_mode=pl.Buffered(1))
    return pl.pallas_call(
        functools.partial(_attn_kernel, t=t, nh=nh),
        grid=(heads // nh, s // t),
        in_specs=[pl.BlockSpec((nh, t, QK_PAD), lambda h, i: (h, i, 0)),
                  resident(QK_PAD), resident(LANES)],
        out_specs=pl.BlockSpec((t, nh * LANES), lambda h, i: (i, h)),
        out_shape=jax.ShapeDtypeStruct((s, heads * LANES), BF16),
        scratch_shapes=[pltpu.VMEM((2, nh, t, t), F32), pltpu.VMEM((2, nh, t, t), BF16),
                        pltpu.VMEM((nh, t, LANES), F32), pltpu.VMEM((nh, t, LANES), F32),
                        pltpu.VMEM((nh, t, LANES), F32)],
        compiler_params=_cparams(("parallel", "arbitrary")),
        name="attention",
    )(q8, k8, v8)


def _layer_norm(y, g, b):
    mu = jnp.mean(y, axis=-1, keepdims=True)
    yc = y - mu
    var = jnp.mean(yc * yc, axis=-1, keepdims=True)
    return yc * lax.rsqrt(var + NORM_EPS) * g + b


def _out_proj_kernel(att_ref, mix_ref, x_ref, wa_ref, wm_ref, g_ref, b_ref, o_ref):
    y = jnp.dot(att_ref[...], wa_ref[...], preferred_element_type=F32)
    y = y + jnp.dot(mix_ref[...], wm_ref[...], preferred_element_type=F32)
    y = y + DEEPNORM_ALPHA * x_ref[...]
    o_ref[...] = _layer_norm(y, g_ref[...], b_ref[...])


def _out_proj(att, mix, x, w_att, w_mix, ln_g, ln_b, tm):
    s, d = x.shape
    row = lambda w: pl.BlockSpec((tm, w), lambda i: (i, 0))
    full = lambda a: pl.BlockSpec(a.shape, lambda i: (0,) * a.ndim)
    return pl.pallas_call(
        _out_proj_kernel,
        grid=(s // tm,),
        in_specs=[row(att.shape[1]), row(mix.shape[1]), row(d), full(w_att), full(w_mix),
                  full(ln_g), full(ln_b)],
        out_specs=row(d),
        out_shape=jax.ShapeDtypeStruct((s, d), F32),
        compiler_params=_cparams(("parallel",)),
        name="out_proj_ln",
    )(att, mix, x, w_att, w_mix, ln_g, ln_b)


def _ffn_kernel(x_ref, wg_ref, wu_ref, wd_ref, g_ref, b_ref, o_ref, xb_ref):
    f = pl.program_id(1)

    @pl.when(f == 0)
    def _():
        x = x_ref[...]
        xb_ref[...] = x.astype(BF16)
        o_ref[...] = DEEPNORM_ALPHA * x

    xb = xb_ref[...]
    gate = jnp.dot(xb, wg_ref[...], preferred_element_type=F32)
    up = jnp.dot(xb, wu_ref[...], preferred_element_type=F32)
    act = (gate * jax.nn.sigmoid(gate) * up).astype(BF16)
    o_ref[...] += jnp.dot(act, wd_ref[...], preferred_element_type=F32)

    @pl.when(f == pl.num_programs(1) - 1)
    def _():
        o_ref[...] = _layer_norm(o_ref[...], g_ref[...], b_ref[...])


def _ffn(x, w_gate, w_up, w_down, ln_g, ln_b, layer, tm, tf):
    s, d = x.shape
    dff = w_gate.shape[2]
    return pl.pallas_call(
        _ffn_kernel,
        grid=(s // tm, dff // tf),
        in_specs=[pl.BlockSpec((tm, d), lambda i, f: (i, 0), pipeline_mode=pl.Buffered(1)),
                  pl.BlockSpec((None, d, tf), lambda i, f: (layer, 0, f)),
                  pl.BlockSpec((None, d, tf), lambda i, f: (layer, 0, f)),
                  pl.BlockSpec((None, tf, d), lambda i, f: (layer, f, 0)),
                  pl.BlockSpec((1, d), lambda i, f: (0, 0)),
                  pl.BlockSpec((1, d), lambda i, f: (0, 0))],
        out_specs=pl.BlockSpec((tm, d), lambda i, f: (i, 0)),
        out_shape=jax.ShapeDtypeStruct((s, d), F32),
        scratch_shapes=[pltpu.VMEM((tm, d), BF16)],
        compiler_params=_cparams(("parallel", "arbitrary")),
        name="ffn_ln",
    )(x, w_gate, w_up, w_down, ln_g, ln_b)


def _moba_permute(w):
    d = w.shape[0]
    w = w.reshape(d, MOBA_HEADS, MOBA_DIM)
    half = MOBA_ROT // 2
    out = jnp.concatenate([w[:, :, :half], w[:, :, MOBA_ROT:MOBA_ROT + 48],
                           w[:, :, half:MOBA_ROT], w[:, :, MOBA_ROT + 48:]], axis=2)
    return out.reshape(d, W_GROUP)


def _relayout_w_in(w_in):
    d = w_in.shape[0]
    c_q, c_kv, k_rope, mq, mk, mv, conv_in, pool_in = jnp.split(
        w_in, [512, 768, 832, 1344, 1856, 2368, 3392], axis=1)
    permute = _moba_permute
    z32 = jnp.zeros((d, 32), w_in.dtype)
    kr = jnp.concatenate([k_rope[:, :32], z32, k_rope[:, 32:], z32], axis=1)
    pad = jnp.zeros((d, IN_PAD - COL_KROPE - LANES), w_in.dtype)
    return jnp.concatenate([conv_in[:, :512], conv_in[:, 512:], c_q, pool_in,
                            permute(mq), permute(mk), mv, c_kv, kr, pad], axis=1)


def _relayout_w_uq(w_uq):
    r = w_uq.shape[0]
    w = w_uq.reshape(r, MLA_HEADS, MLA_QK)
    z32 = jnp.zeros((r, MLA_HEADS, 32), w_uq.dtype)
    half = MLA_ROPE // 2
    out = jnp.concatenate([w[:, :, :MLA_NOPE], w[:, :, MLA_NOPE:MLA_NOPE + half], z32,
                           w[:, :, MLA_NOPE + half:], z32], axis=2)
    return out.reshape(r, MLA_HEADS * QK_PAD)


def _rope_tables(positions):
    pos = positions.astype(F32)[0][:, None]

    def cos_sin(dim):
        inv_freq = 1.0 / (ROPE_THETA ** (jnp.arange(0, dim, 2, dtype=F32) / dim))
        ang = pos * inv_freq
        return jnp.cos(ang), jnp.sin(ang)

    s = pos.shape[0]
    c, sn = cos_sin(MLA_ROPE)
    z = jnp.zeros((s, 32), F32)
    ca = jnp.concatenate([c, z, c, z], axis=1)
    sa = jnp.concatenate([-sn, z, sn, z], axis=1)
    c, sn = cos_sin(MOBA_ROT)
    one = jnp.ones((s, 48), F32)
    z = jnp.zeros((s, 48), F32)
    cm = jnp.concatenate([c, one, c, one], axis=1)
    sm = jnp.concatenate([-sn, z, sn, z], axis=1)
    return ca, sa, cm, sm


def kernel(x, positions, w_in, g_q, w_uq, g_kv, w_ukv, conv_dw_w, conv_dw_b, conv_ln_g,
           conv_ln_b, conv_pw_w, pool_w, pool_scale, w_out, ln1_g, ln1_b, w_gate, w_up,
           w_down, ln2_g, ln2_b):
    b, s, d = x.shape
    assert b == 1 and d == D_MODEL and s % 1024 == 0
    ca, sa, cm, sm = _rope_tables(positions)
    row2 = lambda a: a.reshape(1, -1)
    w_gate_b, w_up_b, w_down_b = w_gate.astype(BF16), w_up.astype(BF16), w_down.astype(BF16)
    h = x[0]
    for l in range(DEPTH):
        w_in_p = _relayout_w_in(w_in[l]).astype(BF16)
        w_uq_p = _relayout_w_uq(w_uq[l]).astype(BF16)
        w_att = w_out[l][:2 * W_GROUP].astype(BF16)
        w_mix = w_out[l][2 * W_GROUP:].astype(BF16)
        dw_w = jnp.pad(conv_dw_w[l], ((0, 1), (0, 0)))

        proj = _matmul(h, w_in_p, F32, tm=1024, tn=1024)
        q8, k8, v8 = _prep_attn(proj, row2(g_q[l]), row2(g_kv[l]), w_uq_p,
                                w_ukv[l].astype(BF16), ca, sa, cm, sm, tr=256)
        mix = _prep_mix(proj, dw_w, row2(conv_dw_b[l]), row2(conv_ln_g[l]),
                        row2(conv_ln_b[l]), conv_pw_w[l].astype(BF16),
                        pool_w[l].astype(BF16), row2(pool_scale[l]), tr=256)
        att = _attention(q8, k8, v8, t=512, nh=2)
        x1 = _out_proj(att, mix, h, w_att, w_mix, row2(ln1_g[l]), row2(ln1_b[l]), tm=512)
        h = _ffn(x1, w_gate_b, w_up_b, w_down_b, row2(ln2_g[l]), row2(ln2_b[l]), layer=l,
                 tm=1024, tf=512)
    return h[None]
```

```python
import functools

import jax
import jax.numpy as jnp
from jax import lax
from jax.experimental import pallas as pl
from jax.experimental.pallas import tpu as pltpu

F32 = jnp.float32
BF16 = jnp.bfloat16

D_MODEL = 2048
W_GROUP = 512
MLA_HEADS = 4
MLA_NOPE = 128
MLA_ROPE = 64
MLA_V = 128
MLA_QK = MLA_NOPE + MLA_ROPE
Q_LORA = 512
KV_LORA = 256
MOBA_HEADS = 4
MOBA_DIM = 128
MOBA_BLOCK = 256
MOBA_TOPK = 3
MOBA_ROT = 32
CONV_WIDTH = 31
POOL_WINDOWS = (2, 4, 8, 16)
POOL_GROUP = 128
ROPE_THETA = 500000.0
NORM_EPS = 1e-5
NEG_INF = -1e30
DEPTH = 2
DEEPNORM_ALPHA = (2 * DEPTH) ** 0.25

LANES = 128
SUBLANES = 8
HEADS = MLA_HEADS + MOBA_HEADS
QK_PAD = 256
LOG2E = 1.4426950408889634
SOFTMAX_ROWS = 64
VMEM_LIMIT = 56 * 1024 * 1024

COL_CONV_A = 0
COL_CONV_G = 512
COL_CQ = 1024
COL_POOL = 1536
COL_MQ = 2048
COL_MK = 2560
COL_MV = 3072
COL_CKV = 3584
COL_KROPE = 3840
IN_PAD = 4096


def _cparams(sem):
    return pltpu.CompilerParams(dimension_semantics=sem, vmem_limit_bytes=VMEM_LIMIT)


def _matmul_kernel(x_ref, w_ref, o_ref):
    o_ref[...] = jnp.dot(x_ref[...].astype(BF16), w_ref[...],
                         preferred_element_type=F32).astype(o_ref.dtype)


def _matmul(x, w, out_dtype, tm, tn):
    m, k = x.shape
    n = w.shape[1]
    return pl.pallas_call(
        _matmul_kernel,
        grid=(m // tm, n // tn),
        in_specs=[pl.BlockSpec((tm, k), lambda i, j: (i, 0)),
                  pl.BlockSpec((k, tn), lambda i, j: (0, j))],
        out_specs=pl.BlockSpec((tm, tn), lambda i, j: (i, j)),
        out_shape=jax.ShapeDtypeStruct((m, n), out_dtype),
        compiler_params=_cparams(("parallel", "parallel")),
        name="in_proj",
    )(x, w)


def _rms(x, g):
    return x * lax.rsqrt(jnp.mean(x * x, axis=-1, keepdims=True) + NORM_EPS) * g


def _rope128(r, c, s):
    return r * c + pltpu.roll(r, 64, 1) * s


def _prep_attn_kernel(cq_ref, mq_ref, mk_ref, mv_ref, ckv_ref, kr_ref,
                      gq_ref, gkv_ref, wuq_ref, wukv_ref,
                      ca_ref, sa_ref, cm_ref, sm_ref,
                      q8_ref, k8_ref, v8_ref, kmean_ref, *, tr):
    i = pl.program_id(0)

    @pl.when(i == 0)
    def _():
        kmean_ref[...] = jnp.zeros_like(kmean_ref)

    ca = ca_ref[...]
    sa = sa_ref[...]
    cm = cm_ref[...]
    sm = sm_ref[...]
    sc_a = MLA_QK ** -0.5 * LOG2E
    sc_b = MOBA_DIM ** -0.5 * LOG2E

    qn = _rms(cq_ref[...], gq_ref[...]).astype(BF16)
    q = jnp.dot(qn, wuq_ref[...], preferred_element_type=F32)
    for h in range(MLA_HEADS):
        base = h * QK_PAD
        q8_ref[h, 0:LANES, :] = (q[:, base:base + LANES] * sc_a).T.astype(BF16)
        pe = _rope128(q[:, base + LANES:base + 2 * LANES], ca, sa)
        q8_ref[h, LANES:2 * LANES, :] = (pe * sc_a).T.astype(BF16)

    kvn = _rms(ckv_ref[...], gkv_ref[...]).astype(BF16)
    kv = jnp.dot(kvn, wukv_ref[...], preferred_element_type=F32)
    kpe = _rope128(kr_ref[...], ca, sa).astype(BF16)
    for h in range(MLA_HEADS):
        base = h * 2 * LANES
        k8_ref[h, :, 0:LANES] = kv[:, base:base + LANES].astype(BF16)
        k8_ref[h, :, LANES:2 * LANES] = kpe
        v8_ref[h] = kv[:, base + LANES:base + 2 * LANES].T.astype(BF16)

    row = i * tr + lax.broadcasted_iota(jnp.int32, (tr, 1), 0)
    own = lax.shift_right_logical(row, 8)
    lane = lax.broadcasted_iota(jnp.int32, (tr, LANES), 1).astype(F32)
    own = own.astype(F32)
    valid = lane < own
    is_own = lane == own
    onehot = jnp.where(is_own, 1.0, 0.0).astype(BF16)
    blocks_per_tile = tr // MOBA_BLOCK
    for h in range(MOBA_HEADS):
        sl = slice(h * LANES, (h + 1) * LANES)
        kb = _rope128(mk_ref[:, sl], cm, sm)
        qb = _rope128(mq_ref[:, sl], cm, sm)
        for b in range(blocks_per_tile):
            ksum = jnp.sum(kb[b * MOBA_BLOCK:(b + 1) * MOBA_BLOCK], axis=0, keepdims=True)
            kmean_ref[h, pl.ds(i * blocks_per_tile + b, 1), :] = ksum * (1.0 / MOBA_BLOCK)
        gate = lax.dot_general(qb, kmean_ref[h], (((1,), (1,)), ((), ())),
                               precision=lax.Precision.HIGHEST,
                               preferred_element_type=F32)
        g = jnp.where(valid, gate, NEG_INF)
        sel = jnp.zeros((tr, LANES), jnp.bool_)
        for _ in range(MOBA_TOPK):
            mx = jnp.max(g, axis=1, keepdims=True)
            idx = jnp.min(jnp.where(g == mx, lane, float(LANES)), axis=1, keepdims=True)
            hit = lane == idx
            sel = jnp.logical_or(sel, hit)
            g = jnp.where(hit, -3.0e38, g)
        ok = jnp.logical_or(jnp.logical_and(sel, valid), is_own)
        bias = jnp.where(ok, 0.0, NEG_INF)
        q8_ref[MLA_HEADS + h, 0:LANES, :] = (qb * sc_b).T.astype(BF16)
        q8_ref[MLA_HEADS + h, LANES:2 * LANES, :] = bias.T.astype(BF16)
        k8_ref[MLA_HEADS + h, :, 0:LANES] = kb.astype(BF16)
        k8_ref[MLA_HEADS + h, :, LANES:2 * LANES] = onehot
        v8_ref[MLA_HEADS + h] = mv_ref[:, sl].T.astype(BF16)


def _prep_attn(h, g_q, g_kv, w_uq_p, w_ukv, ca, sa, cm, sm, tr):
    s = h.shape[0]
    hb = lambda w, c: pl.BlockSpec((tr, w), lambda i, c=c: (i, c))
    full = lambda a: pl.BlockSpec(a.shape, lambda i: (0,) * a.ndim)
    tab = pl.BlockSpec((tr, LANES), lambda i: (i, 0))
    out3 = lambda w: pl.BlockSpec((HEADS, tr, w), lambda i: (0, i, 0))
    out3t = lambda w: pl.BlockSpec((HEADS, w, tr), lambda i: (0, 0, i))
    return pl.pallas_call(
        functools.partial(_prep_attn_kernel, tr=tr),
        grid=(s // tr,),
        in_specs=[hb(512, COL_CQ // 512), hb(512, COL_MQ // 512), hb(512, COL_MK // 512),
                  hb(512, COL_MV // 512), hb(256, COL_CKV // 256), hb(128, COL_KROPE // 128),
                  full(g_q), full(g_kv), full(w_uq_p), full(w_ukv), tab, tab, tab, tab],
        out_specs=[out3t(QK_PAD), out3(QK_PAD), out3t(LANES)],
        out_shape=[jax.ShapeDtypeStruct((HEADS, QK_PAD, s), BF16),
                   jax.ShapeDtypeStruct((HEADS, s, QK_PAD), BF16),
                   jax.ShapeDtypeStruct((HEADS, LANES, s), BF16)],
        scratch_shapes=[pltpu.VMEM((MOBA_HEADS, LANES, LANES), F32)],
        compiler_params=_cparams(("arbitrary",)),
        name="prep_attn",
    )(h, h, h, h, h, h, g_q, g_kv, w_uq_p, w_ukv, ca, sa, cm, sm)


CONV_ROWS = 64
CONV_HALO = 32
POOL_HALO = 16


def _prep_mix_kernel(a_ref, g_ref, p_ref, dww_ref, dwb_ref, lng_ref, lnb_ref, pww_ref,
                     poolw_ref, pscale_ref, o_ref, hbuf, pbuf, shift_buf, y_buf, *, tr):
    i = pl.program_id(0)

    @pl.when(i == 0)
    def _():
        hbuf[0:CONV_HALO, :] = jnp.zeros((CONV_HALO, W_GROUP), F32)
        pbuf[0:POOL_HALO, :] = jnp.zeros((POOL_HALO, W_GROUP), F32)

    @pl.when(i > 0)
    def _():
        hbuf[0:CONV_HALO, :] = hbuf[tr:tr + CONV_HALO, :]
        pbuf[0:POOL_HALO, :] = pbuf[tr:tr + POOL_HALO, :]

    hdn = a_ref[...] * jax.nn.sigmoid(g_ref[...])
    hbuf[CONV_HALO:CONV_HALO + tr, :] = hdn
    first = CONV_HALO - (CONV_WIDTH - 1)
    for c in range(W_GROUP // LANES):
        cs = slice(c * LANES, (c + 1) * LANES)
        shifted = shift_buf.at[c % 2]
        for b in range(SUBLANES):
            span = tr + CONV_WIDTH - 1 - b - (CONV_WIDTH - 1 - b) % SUBLANES
            shifted[b, 0:span, :] = hbuf[first + b:first + b + span, cs]
        for r in range(0, tr, CONV_ROWS):
            acc = jnp.zeros((CONV_ROWS, LANES), F32) + dwb_ref[:, cs]
            for j in range(CONV_WIDTH):
                b = j % SUBLANES
                acc = acc + dww_ref[j:j + 1, cs] * shifted[b, r + j - b:r + j - b + CONV_ROWS, :]
            y_buf[r:r + CONV_ROWS, cs] = acc
    y = y_buf[...]
    mu = jnp.mean(y, axis=-1, keepdims=True)
    yc = y - mu
    var = jnp.mean(yc * yc, axis=-1, keepdims=True)
    z = yc * lax.rsqrt(var + NORM_EPS) * lng_ref[...] + lnb_ref[...]
    z = z * jax.nn.sigmoid(z)
    o_ref[:, 0:W_GROUP] = jnp.dot(z.astype(BF16), pww_ref[...],
                                  preferred_element_type=F32).astype(o_ref.dtype)

    u = p_ref[...]
    pbuf[POOL_HALO:POOL_HALO + tr, :] = u
    t = i * tr + lax.broadcasted_iota(jnp.int32, (tr, 1), 0)
    for gi, w in enumerate(POOL_WINDOWS):
        sl = slice(gi * POOL_GROUP, (gi + 1) * POOL_GROUP)
        ssum = u[:, sl]
        for d in range(1, w):
            ssum = ssum + pbuf[POOL_HALO - d:POOL_HALO - d + tr, sl]
        count = jnp.minimum(t + 1, w).astype(F32)
        pooled = ssum / count - u[:, sl]
        mixed = jnp.dot(pooled.astype(BF16), poolw_ref[gi], preferred_element_type=F32)
        o_ref[:, W_GROUP + gi * POOL_GROUP:W_GROUP + (gi + 1) * POOL_GROUP] = (
            mixed * pscale_ref[:, sl]).astype(o_ref.dtype)


def _prep_mix(h, dw_w, dw_b, ln_g, ln_b, pw_w, pool_w, pool_scale, tr):
    s = h.shape[0]
    hb = lambda c: pl.BlockSpec((tr, 512), lambda i, c=c: (i, c))
    full = lambda a: pl.BlockSpec(a.shape, lambda i: (0,) * a.ndim)
    return pl.pallas_call(
        functools.partial(_prep_mix_kernel, tr=tr),
        grid=(s // tr,),
        in_specs=[hb(COL_CONV_A // 512), hb(COL_CONV_G // 512), hb(COL_POOL // 512),
                  full(dw_w), full(dw_b), full(ln_g), full(ln_b), full(pw_w),
                  full(pool_w), full(pool_scale)],
        out_specs=pl.BlockSpec((tr, 2 * W_GROUP), lambda i: (i, 0)),
        out_shape=jax.ShapeDtypeStruct((s, 2 * W_GROUP), BF16),
        scratch_shapes=[pltpu.VMEM((CONV_HALO + tr, W_GROUP), F32),
                        pltpu.VMEM((POOL_HALO + tr, W_GROUP), F32),
                        pltpu.VMEM((2, SUBLANES, CONV_HALO + tr, LANES), F32),
                        pltpu.VMEM((tr, W_GROUP), F32)],
        compiler_params=_cparams(("arbitrary",)),
        name="prep_mix",
    )(h, h, h, dw_w, dw_b, ln_g, ln_b, pw_w, pool_w, pool_scale)


def _attn_kernel(q_ref, k_ref, v_ref, o_ref, s_buf, p_buf, m_ref, l_ref, acc_ref, *, t, nh):
    i = pl.program_id(1)
    n = i + 1

    def scores(hd, j, slot):
        k = k_ref[hd, pl.ds(pl.multiple_of(j * t, t), t), :]
        s_buf[slot, hd] = jnp.dot(k, q_ref[hd], preferred_element_type=F32)

    def softmax(hd, slot, diagonal, pv):
        for c in range(t // LANES):
            cols = slice(c * LANES, (c + 1) * LANES)

            def block(r):
                blk = s_buf[slot, hd, r:r + SOFTMAX_ROWS, cols]
                if diagonal:
                    key = r + lax.broadcasted_iota(jnp.int32, (SOFTMAX_ROWS, LANES), 0)
                    qry = c * LANES + lax.broadcasted_iota(jnp.int32, (SOFTMAX_ROWS, LANES), 1)
                    blk = jnp.where(key <= qry, blk, NEG_INF)
                return blk.reshape(SOFTMAX_ROWS // SUBLANES, SUBLANES, LANES)

            mx8 = functools.reduce(jnp.maximum,
                                   [jnp.max(block(r), axis=0) for r in range(0, t, SOFTMAX_ROWS)])
            m_old = m_ref[hd, :, cols]
            m_new = jnp.maximum(m_old, jnp.max(mx8, axis=0, keepdims=True))
            alpha = jnp.exp2(m_old - m_new)
            total = jnp.zeros((SUBLANES, LANES), F32)
            for r in range(0, t, SOFTMAX_ROWS):
                p = jnp.exp2(block(r) - m_new[None])
                total = total + jnp.sum(p, axis=0)
                p_buf[slot, hd, r:r + SOFTMAX_ROWS, cols] = p.reshape(SOFTMAX_ROWS, LANES).astype(BF16)
            l_ref[hd, :, cols] = alpha * l_ref[hd, :, cols] + total
            m_ref[hd, :, cols] = m_new
            if pv is not None:
                acc = (acc_ref[hd, :, cols] + pv[:, cols]).reshape(LANES // SUBLANES, SUBLANES, LANES)
                acc_ref[hd, :, cols] = (acc * alpha[None]).reshape(LANES, LANES)

    def values(hd, j, slot):
        v = v_ref[hd, :, pl.ds(pl.multiple_of(j * t, t), t)]
        return jnp.dot(v, p_buf[slot, hd], preferred_element_type=F32)

    def step(j, slot, *, ahead, behind, diagonal):
        for hd in range(nh):
            pv = values(hd, j - 1, 1 - slot) if behind else None
            if ahead:
                scores(hd, j + 1, 1 - slot)
            softmax(hd, slot, diagonal, pv)

    def finish(j, slot):
        for hd in range(nh):
            acc = acc_ref[hd] + values(hd, j, slot)
            out = acc / jnp.sum(l_ref[hd], axis=0, keepdims=True)
            o_ref[:, hd * LANES:(hd + 1) * LANES] = out.T.astype(o_ref.dtype)

    m_ref[...] = jnp.full(m_ref.shape, NEG_INF, F32)
    l_ref[...] = jnp.zeros(l_ref.shape, F32)
    acc_ref[...] = jnp.zeros(acc_ref.shape, F32)
    for hd in range(nh):
        scores(hd, 0, 0)

    @pl.when(n == 1)
    def _():
        step(0, 0, ahead=False, behind=False, diagonal=True)
        finish(0, 0)

    @pl.when(n > 1)
    def _():
        step(0, 0, ahead=True, behind=False, diagonal=False)

        def body(u, carry):
            j = 2 * u + 1
            step(j, 1, ahead=True, behind=True, diagonal=False)
            step(j + 1, 0, ahead=True, behind=True, diagonal=False)
            return carry

        lax.fori_loop(0, lax.div(n - 2, 2), body, 0)

        @pl.when(lax.rem(n, 2) == 1)
        def _():
            step(n - 2, 1, ahead=True, behind=True, diagonal=False)
            step(n - 1, 0, ahead=False, behind=True, diagonal=True)
            finish(n - 1, 0)

        @pl.when(lax.rem(n, 2) == 0)
        def _():
            step(n - 1, 1, ahead=False, behind=True, diagonal=True)
            finish(n - 1, 1)


def _attention(q8, k8, v8, t, nh):
    heads, s, _ = k8.shape
    resident = lambda shape: pl.BlockSpec(shape, lambda h, i: (h, 0, 0),
                                          pipeline_mode=pl.Buffered(1))
    return pl.pallas_call(
        functools.partial(_attn_kernel, t=t, nh=nh),
        grid=(heads // nh, s // t),
        in_specs=[pl.BlockSpec((nh, QK_PAD, t), lambda h, i: (h, 0, i)),
                  resident((nh, s, QK_PAD)), resident((nh, LANES, s))],
        out_specs=pl.BlockSpec((t, nh * LANES), lambda h, i: (i, h)),
        out_shape=jax.ShapeDtypeStruct((s, heads * LANES), BF16),
        scratch_shapes=[pltpu.VMEM((2, nh, t, t), F32), pltpu.VMEM((2, nh, t, t), BF16),
                        pltpu.VMEM((nh, SUBLANES, t), F32), pltpu.VMEM((nh, SUBLANES, t), F32),
                        pltpu.VMEM((nh, LANES, t), F32)],
        compiler_params=_cparams(("parallel", "arbitrary")),
        name="attention",
    )(q8, k8, v8)


def _layer_norm(y, g, b):
    mu = jnp.mean(y, axis=-1, keepdims=True)
    yc = y - mu
    var = jnp.mean(yc * yc, axis=-1, keepdims=True)
    return yc * lax.rsqrt(var + NORM_EPS) * g + b


def _out_proj_kernel(att_ref, mix_ref, x_ref, wa_ref, wm_ref, g_ref, b_ref, o_ref):
    y = jnp.dot(att_ref[...], wa_ref[...], preferred_element_type=F32)
    y = y + jnp.dot(mix_ref[...], wm_ref[...], preferred_element_type=F32)
    y = y + DEEPNORM_ALPHA * x_ref[...]
    o_ref[...] = _layer_norm(y, g_ref[...], b_ref[...])


def _out_proj(att, mix, x, w_att, w_mix, ln_g, ln_b, tm):
    s, d = x.shape
    row = lambda w: pl.BlockSpec((tm, w), lambda i: (i, 0))
    full = lambda a: pl.BlockSpec(a.shape, lambda i: (0,) * a.ndim)
    return pl.pallas_call(
        _out_proj_kernel,
        grid=(s // tm,),
        in_specs=[row(att.shape[1]), row(mix.shape[1]), row(d), full(w_att), full(w_mix),
                  full(ln_g), full(ln_b)],
        out_specs=row(d),
        out_shape=jax.ShapeDtypeStruct((s, d), F32),
        compiler_params=_cparams(("parallel",)),
        name="out_proj_ln",
    )(att, mix, x, w_att, w_mix, ln_g, ln_b)


def _ffn_kernel(x_ref, wg_ref, wu_ref, wd_ref, g_ref, b_ref, o_ref, xb_ref):
    f = pl.program_id(1)

    @pl.when(f == 0)
    def _():
        x = x_ref[...]
        xb_ref[...] = x.astype(BF16)
        o_ref[...] = DEEPNORM_ALPHA * x

    xb = xb_ref[...]
    gate = jnp.dot(xb, wg_ref[...], preferred_element_type=F32)
    up = jnp.dot(xb, wu_ref[...], preferred_element_type=F32)
    act = (gate * jax.nn.sigmoid(gate) * up).astype(BF16)
    o_ref[...] += jnp.dot(act, wd_ref[...], preferred_element_type=F32)

    @pl.when(f == pl.num_programs(1) - 1)
    def _():
        o_ref[...] = _layer_norm(o_ref[...], g_ref[...], b_ref[...])


def _ffn(x, w_gate, w_up, w_down, ln_g, ln_b, layer, tm, tf):
    s, d = x.shape
    dff = w_gate.shape[2]
    return pl.pallas_call(
        _ffn_kernel,
        grid=(s // tm, dff // tf),
        in_specs=[pl.BlockSpec((tm, d), lambda i, f: (i, 0)),
                  pl.BlockSpec((None, d, tf), lambda i, f: (layer, 0, f)),
                  pl.BlockSpec((None, d, tf), lambda i, f: (layer, 0, f)),
                  pl.BlockSpec((None, tf, d), lambda i, f: (layer, f, 0)),
                  pl.BlockSpec((1, d), lambda i, f: (0, 0)),
                  pl.BlockSpec((1, d), lambda i, f: (0, 0))],
        out_specs=pl.BlockSpec((tm, d), lambda i, f: (i, 0)),
        out_shape=jax.ShapeDtypeStruct((s, d), F32),
        scratch_shapes=[pltpu.VMEM((tm, d), BF16)],
        compiler_params=_cparams(("parallel", "arbitrary")),
        name="ffn_ln",
    )(x, w_gate, w_up, w_down, ln_g, ln_b)


def _moba_permute(w):
    d = w.shape[0]
    w = w.reshape(d, MOBA_HEADS, MOBA_DIM)
    half = MOBA_ROT // 2
    out = jnp.concatenate([w[:, :, :half], w[:, :, MOBA_ROT:MOBA_ROT + 48],
                           w[:, :, half:MOBA_ROT], w[:, :, MOBA_ROT + 48:]], axis=2)
    return out.reshape(d, W_GROUP)


def _relayout_w_in(w_in):
    d = w_in.shape[0]
    c_q, c_kv, k_rope, mq, mk, mv, conv_in, pool_in = jnp.split(
        w_in, [512, 768, 832, 1344, 1856, 2368, 3392], axis=1)
    permute = _moba_permute
    z32 = jnp.zeros((d, 32), w_in.dtype)
    kr = jnp.concatenate([k_rope[:, :32], z32, k_rope[:, 32:], z32], axis=1)
    pad = jnp.zeros((d, IN_PAD - COL_KROPE - LANES), w_in.dtype)
    return jnp.concatenate([conv_in[:, :512], conv_in[:, 512:], c_q, pool_in,
                            permute(mq), permute(mk), mv, c_kv, kr, pad], axis=1)


def _relayout_w_uq(w_uq):
    r = w_uq.shape[0]
    w = w_uq.reshape(r, MLA_HEADS, MLA_QK)
    z32 = jnp.zeros((r, MLA_HEADS, 32), w_uq.dtype)
    half = MLA_ROPE // 2
    out = jnp.concatenate([w[:, :, :MLA_NOPE], w[:, :, MLA_NOPE:MLA_NOPE + half], z32,
                           w[:, :, MLA_NOPE + half:], z32], axis=2)
    return out.reshape(r, MLA_HEADS * QK_PAD)


def _rope_tables(positions):
    pos = positions.astype(F32)[0][:, None]

    def cos_sin(dim):
        inv_freq = 1.0 / (ROPE_THETA ** (jnp.arange(0, dim, 2, dtype=F32) / dim))
        ang = (pos * inv_freq).reshape(-1, LANES)
        return jnp.cos(ang).reshape(-1, dim // 2), jnp.sin(ang).reshape(-1, dim // 2)

    s = pos.shape[0]
    c, sn = cos_sin(MLA_ROPE)
    z = jnp.zeros((s, 32), F32)
    ca = jnp.concatenate([c, z, c, z], axis=1)
    sa = jnp.concatenate([-sn, z, sn, z], axis=1)
    c, sn = cos_sin(MOBA_ROT)
    one = jnp.ones((s, 48), F32)
    z = jnp.zeros((s, 48), F32)
    cm = jnp.concatenate([c, one, c, one], axis=1)
    sm = jnp.concatenate([-sn, z, sn, z], axis=1)
    return ca, sa, cm, sm


def kernel(x, positions, w_in, g_q, w_uq, g_kv, w_ukv, conv_dw_w, conv_dw_b, conv_ln_g,
           conv_ln_b, conv_pw_w, pool_w, pool_scale, w_out, ln1_g, ln1_b, w_gate, w_up,
           w_down, ln2_g, ln2_b):
    b, s, d = x.shape
    assert b == 1 and d == D_MODEL and s % 1024 == 0
    ca, sa, cm, sm = _rope_tables(positions)
    row2 = lambda a: a.reshape(1, -1)
    w_gate_b, w_up_b, w_down_b = w_gate.astype(BF16), w_up.astype(BF16), w_down.astype(BF16)
    h = x[0]
    for l in range(DEPTH):
        w_in_p = _relayout_w_in(w_in[l].astype(BF16))
        w_uq_p = _relayout_w_uq(w_uq[l]).astype(BF16)
        w_att = w_out[l][:2 * W_GROUP].astype(BF16)
        w_mix = w_out[l][2 * W_GROUP:].astype(BF16)
        dw_w = jnp.pad(conv_dw_w[l], ((0, 1), (0, 0)))

        proj = _matmul(h, w_in_p, F32, tm=1024, tn=1024)
        q8, k8, v8 = _prep_attn(proj, row2(g_q[l]), row2(g_kv[l]), w_uq_p,
                                w_ukv[l].astype(BF16), ca, sa, cm, sm, tr=256)
        mix = _prep_mix(proj, dw_w, row2(conv_dw_b[l]), row2(conv_ln_g[l]),
                        row2(conv_ln_b[l]), conv_pw_w[l].astype(BF16),
                        pool_w[l].astype(BF16), row2(pool_scale[l]), tr=256)
        att = _attention(q8, k8, v8, t=512, nh=2)
        x1 = _out_proj(att, mix, h, w_att, w_mix, row2(ln1_g[l]), row2(ln1_b[l]), tm=512)
        h = _ffn(x1, w_gate_b, w_up_b, w_down_b, row2(ln2_g[l]), row2(ln2_b[l]), layer=l,
                 tm=512, tf=512)
    return h[None]
```

```python
import functools

import jax
import jax.numpy as jnp
from jax import lax
from jax.experimental import pallas as pl
from jax.experimental.pallas import tpu as pltpu

F32 = jnp.float32
BF16 = jnp.bfloat16

D_MODEL = 2048
W_GROUP = 512
MLA_HEADS = 4
MLA_NOPE = 128
MLA_ROPE = 64
MLA_V = 128
MLA_QK = MLA_NOPE + MLA_ROPE
Q_LORA = 512
KV_LORA = 256
MOBA_HEADS = 4
MOBA_DIM = 128
MOBA_BLOCK = 256
MOBA_TOPK = 3
MOBA_ROT = 32
CONV_WIDTH = 31
POOL_WINDOWS = (2, 4, 8, 16)
POOL_GROUP = 128
ROPE_THETA = 500000.0
NORM_EPS = 1e-5
NEG_INF = -1e30
DEPTH = 2
DEEPNORM_ALPHA = (2 * DEPTH) ** 0.25

LANES = 128
SUBLANES = 8
HEADS = MLA_HEADS + MOBA_HEADS
QK_PAD = 256
LOG2E = 1.4426950408889634
SOFTMAX_ROWS = 64
VMEM_LIMIT = 56 * 1024 * 1024

COL_CONV_A = 0
COL_CONV_G = 512
COL_CQ = 1024
COL_POOL = 1536
COL_MQ = 2048
COL_MK = 2560
COL_MV = 3072
COL_CKV = 3584
COL_KROPE = 3840
IN_PAD = 4096


def _cparams(sem):
    return pltpu.CompilerParams(dimension_semantics=sem, vmem_limit_bytes=VMEM_LIMIT)


def _matmul_kernel(x_ref, w_ref, o_ref):
    o_ref[...] = jnp.dot(x_ref[...].astype(BF16), w_ref[...],
                         preferred_element_type=F32).astype(o_ref.dtype)


def _matmul(x, w, out_dtype, tm, tn):
    m, k = x.shape
    n = w.shape[1]
    return pl.pallas_call(
        _matmul_kernel,
        grid=(m // tm, n // tn),
        in_specs=[pl.BlockSpec((tm, k), lambda i, j: (i, 0)),
                  pl.BlockSpec((k, tn), lambda i, j: (0, j))],
        out_specs=pl.BlockSpec((tm, tn), lambda i, j: (i, j)),
        out_shape=jax.ShapeDtypeStruct((m, n), out_dtype),
        compiler_params=_cparams(("parallel", "parallel")),
        name="in_proj",
    )(x, w)


def _rms(x, g):
    return x * lax.rsqrt(jnp.mean(x * x, axis=-1, keepdims=True) + NORM_EPS) * g


def _rope128(r, c, s):
    return r * c + pltpu.roll(r, 64, 1) * s


def _prep_attn_kernel(cq_ref, mq_ref, mk_ref, mv_ref, ckv_ref, kr_ref,
                      gq_ref, gkv_ref, wuq_ref, wukv_ref,
                      ca_ref, sa_ref, cm_ref, sm_ref,
                      q8_ref, k8_ref, v8_ref, kmean_ref, *, tr):
    i = pl.program_id(0)

    @pl.when(i == 0)
    def _():
        kmean_ref[...] = jnp.zeros_like(kmean_ref)

    ca = ca_ref[...]
    sa = sa_ref[...]
    cm = cm_ref[...]
    sm = sm_ref[...]
    sc_a = MLA_QK ** -0.5 * LOG2E
    sc_b = MOBA_DIM ** -0.5 * LOG2E

    qn = _rms(cq_ref[...], gq_ref[...]).astype(BF16)
    q = jnp.dot(qn, wuq_ref[...], preferred_element_type=F32)
    for h in range(MLA_HEADS):
        base = h * QK_PAD
        q8_ref[h, :, 0:LANES] = (q[:, base:base + LANES] * sc_a).astype(BF16)
        pe = _rope128(q[:, base + LANES:base + 2 * LANES], ca, sa)
        q8_ref[h, :, LANES:2 * LANES] = (pe * sc_a).astype(BF16)

    kvn = _rms(ckv_ref[...], gkv_ref[...]).astype(BF16)
    kv = jnp.dot(kvn, wukv_ref[...], preferred_element_type=F32)
    kpe = _rope128(kr_ref[...], ca, sa).T.astype(BF16)
    for h in range(MLA_HEADS):
        base = h * 2 * LANES
        k8_ref[h, 0:LANES, :] = kv[:, base:base + LANES].T.astype(BF16)
        k8_ref[h, LANES:2 * LANES, :] = kpe
        v8_ref[h] = kv[:, base + LANES:base + 2 * LANES].astype(BF16)

    row = i * tr + lax.broadcasted_iota(jnp.int32, (tr, 1), 0)
    own = lax.shift_right_logical(row, 8)
    lane = lax.broadcasted_iota(jnp.int32, (tr, LANES), 1).astype(F32)
    own = own.astype(F32)
    valid = lane < own
    is_own = lane == own
    onehot = jnp.where(is_own, 1.0, 0.0).T.astype(BF16)
    blocks_per_tile = tr // MOBA_BLOCK
    for h in range(MOBA_HEADS):
        sl = slice(h * LANES, (h + 1) * LANES)
        kb = _rope128(mk_ref[:, sl], cm, sm)
        qb = _rope128(mq_ref[:, sl], cm, sm)
        for b in range(blocks_per_tile):
            ksum = jnp.sum(kb[b * MOBA_BLOCK:(b + 1) * MOBA_BLOCK], axis=0, keepdims=True)
            kmean_ref[h, pl.ds(i * blocks_per_tile + b, 1), :] = ksum * (1.0 / MOBA_BLOCK)
        gate = lax.dot_general(qb, kmean_ref[h], (((1,), (1,)), ((), ())),
                               precision=lax.Precision.HIGHEST,
                               preferred_element_type=F32)
        g = jnp.where(valid, gate, NEG_INF)
        sel = jnp.zeros((tr, LANES), jnp.bool_)
        for _ in range(MOBA_TOPK):
            mx = jnp.max(g, axis=1, keepdims=True)
            idx = jnp.min(jnp.where(g == mx, lane, float(LANES)), axis=1, keepdims=True)
            hit = lane == idx
            sel = jnp.logical_or(sel, hit)
            g = jnp.where(hit, -3.0e38, g)
        ok = jnp.logical_or(jnp.logical_and(sel, valid), is_own)
        bias = jnp.where(ok, 0.0, NEG_INF)
        q8_ref[MLA_HEADS + h, :, 0:LANES] = (qb * sc_b).astype(BF16)
        q8_ref[MLA_HEADS + h, :, LANES:2 * LANES] = bias.astype(BF16)
        k8_ref[MLA_HEADS + h, 0:LANES, :] = kb.T.astype(BF16)
        k8_ref[MLA_HEADS + h, LANES:2 * LANES, :] = onehot
        v8_ref[MLA_HEADS + h] = mv_ref[:, sl].astype(BF16)


def _prep_attn(h, g_q, g_kv, w_uq_p, w_ukv, ca, sa, cm, sm, tr):
    s = h.shape[0]
    hb = lambda w, c: pl.BlockSpec((tr, w), lambda i, c=c: (i, c))
    full = lambda a: pl.BlockSpec(a.shape, lambda i: (0,) * a.ndim)
    tab = pl.BlockSpec((tr, LANES), lambda i: (i, 0))
    out3 = lambda w: pl.BlockSpec((HEADS, tr, w), lambda i: (0, i, 0))
    out3t = lambda w: pl.BlockSpec((HEADS, w, tr), lambda i: (0, 0, i))
    return pl.pallas_call(
        functools.partial(_prep_attn_kernel, tr=tr),
        grid=(s // tr,),
        in_specs=[hb(512, COL_CQ // 512), hb(512, COL_MQ // 512), hb(512, COL_MK // 512),
                  hb(512, COL_MV // 512), hb(256, COL_CKV // 256), hb(128, COL_KROPE // 128),
                  full(g_q), full(g_kv), full(w_uq_p), full(w_ukv), tab, tab, tab, tab],
        out_specs=[out3(QK_PAD), out3t(QK_PAD), out3(LANES)],
        out_shape=[jax.ShapeDtypeStruct((HEADS, s, QK_PAD), BF16),
                   jax.ShapeDtypeStruct((HEADS, QK_PAD, s), BF16),
                   jax.ShapeDtypeStruct((HEADS, s, LANES), BF16)],
        scratch_shapes=[pltpu.VMEM((MOBA_HEADS, LANES, LANES), F32)],
        compiler_params=_cparams(("arbitrary",)),
        name="prep_attn",
    )(h, h, h, h, h, h, g_q, g_kv, w_uq_p, w_ukv, ca, sa, cm, sm)


CONV_ROWS = 64
CONV_HALO = 32
POOL_HALO = 16


def _prep_mix_kernel(a_ref, g_ref, p_ref, dww_ref, dwb_ref, lng_ref, lnb_ref, pww_ref,
                     poolw_ref, pscale_ref, o_ref, hbuf, pbuf, shift_buf, y_buf, *, tr):
    i = pl.program_id(0)

    @pl.when(i == 0)
    def _():
        hbuf[0:CONV_HALO, :] = jnp.zeros((CONV_HALO, W_GROUP), F32)
        pbuf[0:POOL_HALO, :] = jnp.zeros((POOL_HALO, W_GROUP), F32)

    @pl.when(i > 0)
    def _():
        hbuf[0:CONV_HALO, :] = hbuf[tr:tr + CONV_HALO, :]
        pbuf[0:POOL_HALO, :] = pbuf[tr:tr + POOL_HALO, :]

    hdn = a_ref[...] * jax.nn.sigmoid(g_ref[...])
    hbuf[CONV_HALO:CONV_HALO + tr, :] = hdn
    first = CONV_HALO - (CONV_WIDTH - 1)
    for c in range(W_GROUP // LANES):
        cs = slice(c * LANES, (c + 1) * LANES)
        shifted = shift_buf.at[c % 2]
        for b in range(SUBLANES):
            span = tr + CONV_WIDTH - 1 - b - (CONV_WIDTH - 1 - b) % SUBLANES
            shifted[b, 0:span, :] = hbuf[first + b:first + b + span, cs]
        for r in range(0, tr, CONV_ROWS):
            acc = jnp.zeros((CONV_ROWS, LANES), F32) + dwb_ref[:, cs]
            for j in range(CONV_WIDTH):
                b = j % SUBLANES
                acc = acc + dww_ref[j:j + 1, cs] * shifted[b, r + j - b:r + j - b + CONV_ROWS, :]
            y_buf[r:r + CONV_ROWS, cs] = acc
    y = y_buf[...]
    mu = jnp.mean(y, axis=-1, keepdims=True)
    yc = y - mu
    var = jnp.mean(yc * yc, axis=-1, keepdims=True)
    z = yc * lax.rsqrt(var + NORM_EPS) * lng_ref[...] + lnb_ref[...]
    z = z * jax.nn.sigmoid(z)
    o_ref[:, 0:W_GROUP] = jnp.dot(z.astype(BF16), pww_ref[...],
                                  preferred_element_type=F32).astype(o_ref.dtype)

    u = p_ref[...]
    pbuf[POOL_HALO:POOL_HALO + tr, :] = u
    t = i * tr + lax.broadcasted_iota(jnp.int32, (tr, 1), 0)
    for gi, w in enumerate(POOL_WINDOWS):
        sl = slice(gi * POOL_GROUP, (gi + 1) * POOL_GROUP)
        ssum = u[:, sl]
        for d in range(1, w):
            ssum = ssum + pbuf[POOL_HALO - d:POOL_HALO - d + tr, sl]
        count = jnp.minimum(t + 1, w).astype(F32)
        pooled = ssum / count - u[:, sl]
        mixed = jnp.dot(pooled.astype(BF16), poolw_ref[gi], preferred_element_type=F32)
        o_ref[:, W_GROUP + gi * POOL_GROUP:W_GROUP + (gi + 1) * POOL_GROUP] = (
            mixed * pscale_ref[:, sl]).astype(o_ref.dtype)


def _prep_mix(h, dw_w, dw_b, ln_g, ln_b, pw_w, pool_w, pool_scale, tr):
    s = h.shape[0]
    hb = lambda c: pl.BlockSpec((tr, 512), lambda i, c=c: (i, c))
    full = lambda a: pl.BlockSpec(a.shape, lambda i: (0,) * a.ndim)
    return pl.pallas_call(
        functools.partial(_prep_mix_kernel, tr=tr),
        grid=(s // tr,),
        in_specs=[hb(COL_CONV_A // 512), hb(COL_CONV_G // 512), hb(COL_POOL // 512),
                  full(dw_w), full(dw_b), full(ln_g), full(ln_b), full(pw_w),
                  full(pool_w), full(pool_scale)],
        out_specs=pl.BlockSpec((tr, 2 * W_GROUP), lambda i: (i, 0)),
        out_shape=jax.ShapeDtypeStruct((s, 2 * W_GROUP), BF16),
        scratch_shapes=[pltpu.VMEM((CONV_HALO + tr, W_GROUP), F32),
                        pltpu.VMEM((POOL_HALO + tr, W_GROUP), F32),
                        pltpu.VMEM((2, SUBLANES, CONV_HALO + tr, LANES), F32),
                        pltpu.VMEM((tr, W_GROUP), F32)],
        compiler_params=_cparams(("arbitrary",)),
        name="prep_mix",
    )(h, h, h, dw_w, dw_b, ln_g, ln_b, pw_w, pool_w, pool_scale)


def _attn_kernel(q_ref, k_ref, v_ref, o_ref, s_buf, p_buf, m_ref, l_ref, acc_ref, *, t, nh):
    i = pl.program_id(1)
    n = i + 1

    def scores(hd, j, slot):
        k = k_ref[hd, :, pl.ds(pl.multiple_of(j * t, t), t)]
        s_buf[slot, hd] = jnp.dot(q_ref[hd], k, preferred_element_type=F32)

    def softmax(hd, slot, diagonal, pv):
        for r in range(t // SOFTMAX_ROWS):
            rows = slice(r * SOFTMAX_ROWS, (r + 1) * SOFTMAX_ROWS)
            cols = [s_buf[slot, hd, rows, c * LANES:(c + 1) * LANES] for c in range(t // LANES)]
            if diagonal:
                rq = r * SOFTMAX_ROWS + lax.broadcasted_iota(jnp.int32, (SOFTMAX_ROWS, LANES), 0)
                ck = lax.broadcasted_iota(jnp.int32, (SOFTMAX_ROWS, LANES), 1)
                cols = [jnp.where(ck + c * LANES <= rq, s, NEG_INF) for c, s in enumerate(cols)]
            mx = functools.reduce(jnp.maximum, cols)
            m_old = m_ref[hd, rows]
            m_new = jnp.maximum(m_old, jnp.max(mx, axis=1, keepdims=True))
            alpha = jnp.exp2(m_old - m_new)
            ps = [jnp.exp2(s - m_new) for s in cols]
            row = functools.reduce(lambda a, b: a + b, ps)
            l_ref[hd, rows] = alpha * l_ref[hd, rows] + jnp.sum(row, axis=1, keepdims=True)
            m_ref[hd, rows] = m_new
            for c, p in enumerate(ps):
                p_buf[slot, hd, rows, c * LANES:(c + 1) * LANES] = p.astype(BF16)
            if pv is not None:
                acc_ref[hd, rows] = alpha * (acc_ref[hd, rows] + pv[rows])

    def values(hd, j, slot):
        v = v_ref[hd, pl.ds(pl.multiple_of(j * t, t), t), :]
        return jnp.dot(p_buf[slot, hd], v, preferred_element_type=F32)

    def step(j, slot, *, ahead, behind, diagonal):
        for hd in range(nh):
            pv = values(hd, j - 1, 1 - slot) if behind else None
            if ahead:
                scores(hd, j + 1, 1 - slot)
            softmax(hd, slot, diagonal, pv)

    def finish(j, slot):
        for hd in range(nh):
            acc = acc_ref[hd] + values(hd, j, slot)
            o_ref[:, hd * LANES:(hd + 1) * LANES] = (acc / l_ref[hd]).astype(o_ref.dtype)

    m_ref[...] = jnp.full(m_ref.shape, NEG_INF, F32)
    l_ref[...] = jnp.zeros(l_ref.shape, F32)
    acc_ref[...] = jnp.zeros(acc_ref.shape, F32)
    for hd in range(nh):
        scores(hd, 0, 0)

    @pl.when(n == 1)
    def _():
        step(0, 0, ahead=False, behind=False, diagonal=True)
        finish(0, 0)

    @pl.when(n > 1)
    def _():
        step(0, 0, ahead=True, behind=False, diagonal=False)

        def body(u, carry):
            j = 2 * u + 1
            step(j, 1, ahead=True, behind=True, diagonal=False)
            step(j + 1, 0, ahead=True, behind=True, diagonal=False)
            return carry

        lax.fori_loop(0, lax.div(n - 2, 2), body, 0)

        @pl.when(lax.rem(n, 2) == 1)
        def _():
            step(n - 2, 1, ahead=True, behind=True, diagonal=False)
            step(n - 1, 0, ahead=False, behind=True, diagonal=True)
            finish(n - 1, 0)

        @pl.when(lax.rem(n, 2) == 0)
        def _():
            step(n - 1, 1, ahead=False, behind=True, diagonal=True)
            finish(n - 1, 1)


def _attention(q8, k8, v8, t, nh):
    heads, s, _ = q8.shape
    resident = lambda shape: pl.BlockSpec(shape, lambda h, i: (h, 0, 0),
                                          pipeline_mode=pl.Buffered(1))
    return pl.pallas_call(
        functools.partial(_attn_kernel, t=t, nh=nh),
        grid=(heads // nh, s // t),
        in_specs=[pl.BlockSpec((nh, t, QK_PAD), lambda h, i: (h, i, 0)),
                  resident((nh, QK_PAD, s)), resident((nh, s, LANES))],
        out_specs=pl.BlockSpec((t, nh * LANES), lambda h, i: (i, h)),
        out_shape=jax.ShapeDtypeStruct((s, heads * LANES), BF16),
        scratch_shapes=[pltpu.VMEM((2, nh, t, t), F32), pltpu.VMEM((2, nh, t, t), BF16),
                        pltpu.VMEM((nh, t, LANES), F32), pltpu.VMEM((nh, t, LANES), F32),
                        pltpu.VMEM((nh, t, LANES), F32)],
        compiler_params=_cparams(("parallel", "arbitrary")),
        name="attention",
    )(q8, k8, v8)


def _layer_norm(y, g, b):
    mu = jnp.mean(y, axis=-1, keepdims=True)
    yc = y - mu
    var = jnp.mean(yc * yc, axis=-1, keepdims=True)
    return yc * lax.rsqrt(var + NORM_EPS) * g + b


def _out_proj_kernel(att_ref, mix_ref, x_ref, wa_ref, wm_ref, g_ref, b_ref, o_ref):
    y = jnp.dot(att_ref[...], wa_ref[...], preferred_element_type=F32)
    y = y + jnp.dot(mix_ref[...], wm_ref[...], preferred_element_type=F32)
    y = y + DEEPNORM_ALPHA * x_ref[...]
    o_ref[...] = _layer_norm(y, g_ref[...], b_ref[...])


def _out_proj(att, mix, x, w_att, w_mix, ln_g, ln_b, tm):
    s, d = x.shape
    row = lambda w: pl.BlockSpec((tm, w), lambda i: (i, 0))
    full = lambda a: pl.BlockSpec(a.shape, lambda i: (0,) * a.ndim)
    return pl.pallas_call(
        _out_proj_kernel,
        grid=(s // tm,),
        in_specs=[row(att.shape[1]), row(mix.shape[1]), row(d), full(w_att), full(w_mix),
                  full(ln_g), full(ln_b)],
        out_specs=row(d),
        out_shape=jax.ShapeDtypeStruct((s, d), F32),
        compiler_params=_cparams(("parallel",)),
        name="out_proj_ln",
    )(att, mix, x, w_att, w_mix, ln_g, ln_b)


def _ffn_kernel(x_ref, wg_ref, wu_ref, wd_ref, g_ref, b_ref, o_ref, xb_ref):
    f = pl.program_id(1)

    @pl.when(f == 0)
    def _():
        x = x_ref[...]
        xb_ref[...] = x.astype(BF16)
        o_ref[...] = DEEPNORM_ALPHA * x

    xb = xb_ref[...]
    gate = jnp.dot(xb, wg_ref[...], preferred_element_type=F32)
    up = jnp.dot(xb, wu_ref[...], preferred_element_type=F32)
    act = (gate * jax.nn.sigmoid(gate) * up).astype(BF16)
    o_ref[...] += jnp.dot(act, wd_ref[...], preferred_element_type=F32)

    @pl.when(f == pl.num_programs(1) - 1)
    def _():
        o_ref[...] = _layer_norm(o_ref[...], g_ref[...], b_ref[...])


def _ffn(x, w_gate, w_up, w_down, ln_g, ln_b, layer, tm, tf):
    s, d = x.shape
    dff = w_gate.shape[2]
    return pl.pallas_call(
        _ffn_kernel,
        grid=(s // tm, dff // tf),
        in_specs=[pl.BlockSpec((tm, d), lambda i, f: (i, 0)),
                  pl.BlockSpec((None, d, tf), lambda i, f: (layer, 0, f)),
                  pl.BlockSpec((None, d, tf), lambda i, f: (layer, 0, f)),
                  pl.BlockSpec((None, tf, d), lambda i, f: (layer, f, 0)),
                  pl.BlockSpec((1, d), lambda i, f: (0, 0)),
                  pl.BlockSpec((1, d), lambda i, f: (0, 0))],
        out_specs=pl.BlockSpec((tm, d), lambda i, f: (i, 0)),
        out_shape=jax.ShapeDtypeStruct((s, d), F32),
        scratch_shapes=[pltpu.VMEM((tm, d), BF16)],
        compiler_params=_cparams(("parallel", "arbitrary")),
        name="ffn_ln",
    )(x, w_gate, w_up, w_down, ln_g, ln_b)


def _moba_permute(w):
    d = w.shape[0]
    w = w.reshape(d, MOBA_HEADS, MOBA_DIM)
    half = MOBA_ROT // 2
    out = jnp.concatenate([w[:, :, :half], w[:, :, MOBA_ROT:MOBA_ROT + 48],
                           w[:, :, half:MOBA_ROT], w[:, :, MOBA_ROT + 48:]], axis=2)
    return out.reshape(d, W_GROUP)


def _relayout_w_in(w_in):
    d = w_in.shape[0]
    c_q, c_kv, k_rope, mq, mk, mv, conv_in, pool_in = jnp.split(
        w_in, [512, 768, 832, 1344, 1856, 2368, 3392], axis=1)
    permute = _moba_permute
    z32 = jnp.zeros((d, 32), w_in.dtype)
    kr = jnp.concatenate([k_rope[:, :32], z32, k_rope[:, 32:], z32], axis=1)
    pad = jnp.zeros((d, IN_PAD - COL_KROPE - LANES), w_in.dtype)
    return jnp.concatenate([conv_in[:, :512], conv_in[:, 512:], c_q, pool_in,
                            permute(mq), permute(mk), mv, c_kv, kr, pad], axis=1)


def _relayout_w_uq(w_uq):
    r = w_uq.shape[0]
    w = w_uq.reshape(r, MLA_HEADS, MLA_QK)
    z32 = jnp.zeros((r, MLA_HEADS, 32), w_uq.dtype)
    half = MLA_ROPE // 2
    out = jnp.concatenate([w[:, :, :MLA_NOPE], w[:, :, MLA_NOPE:MLA_NOPE + half], z32,
                           w[:, :, MLA_NOPE + half:], z32], axis=2)
    return out.reshape(r, MLA_HEADS * QK_PAD)


def _rope_tables(positions):
    pos = positions.astype(F32)[0][:, None]

    def cos_sin(dim):
        inv_freq = 1.0 / (ROPE_THETA ** (jnp.arange(0, dim, 2, dtype=F32) / dim))
        ang = (pos * inv_freq).reshape(-1, LANES)
        return jnp.cos(ang).reshape(-1, dim // 2), jnp.sin(ang).reshape(-1, dim // 2)

    s = pos.shape[0]
    c, sn = cos_sin(MLA_ROPE)
    z = jnp.zeros((s, 32), F32)
    ca = jnp.concatenate([c, z, c, z], axis=1)
    sa = jnp.concatenate([-sn, z, sn, z], axis=1)
    c, sn = cos_sin(MOBA_ROT)
    one = jnp.ones((s, 48), F32)
    z = jnp.zeros((s, 48), F32)
    cm = jnp.concatenate([c, one, c, one], axis=1)
    sm = jnp.concatenate([-sn, z, sn, z], axis=1)
    return ca, sa, cm, sm


def kernel(x, positions, w_in, g_q, w_uq, g_kv, w_ukv, conv_dw_w, conv_dw_b, conv_ln_g,
           conv_ln_b, conv_pw_w, pool_w, pool_scale, w_out, ln1_g, ln1_b, w_gate, w_up,
           w_down, ln2_g, ln2_b):
    b, s, d = x.shape
    assert b == 1 and d == D_MODEL and s % 1024 == 0
    ca, sa, cm, sm = _rope_tables(positions)
    row2 = lambda a: a.reshape(1, -1)
    w_gate_b, w_up_b, w_down_b = w_gate.astype(BF16), w_up.astype(BF16), w_down.astype(BF16)
    h = x[0]
    for l in range(DEPTH):
        w_in_p = _relayout_w_in(w_in[l]).astype(BF16)
        w_uq_p = _relayout_w_uq(w_uq[l]).astype(BF16)
        w_att = w_out[l][:2 * W_GROUP].astype(BF16)
        w_mix = w_out[l][2 * W_GROUP:].astype(BF16)
        dw_w = jnp.pad(conv_dw_w[l], ((0, 1), (0, 0)))

        proj = _matmul(h, w_in_p, F32, tm=1024, tn=1024)
        q8, k8, v8 = _prep_attn(proj, row2(g_q[l]), row2(g_kv[l]), w_uq_p,
                                w_ukv[l].astype(BF16), ca, sa, cm, sm, tr=256)
        mix = _prep_mix(proj, dw_w, row2(conv_dw_b[l]), row2(conv_ln_g[l]),
                        row2(conv_ln_b[l]), conv_pw_w[l].astype(BF16),
                        pool_w[l].astype(BF16), row2(pool_scale[l]), tr=256)
        att = _attention(q8, k8, v8, t=512, nh=2)
        x1 = _out_proj(att, mix, h, w_att, w_mix, row2(ln1_g[l]), row2(ln1_b[l]), tm=512)
        h = _ffn(x1, w_gate_b, w_up_b, w_down_b, row2(ln2_g[l]), row2(ln2_b[l]), layer=l,
                 tm=512, tf=512)
    return h[None]
```

```python
import functools

import jax
import jax.numpy as jnp
from jax import lax
from jax.experimental import pallas as pl
from jax.experimental.pallas import tpu as pltpu

F32 = jnp.float32
BF16 = jnp.bfloat16

D_MODEL = 2048
W_GROUP = 512
MLA_HEADS = 4
MLA_NOPE = 128
MLA_ROPE = 64
MLA_V = 128
MLA_QK = MLA_NOPE + MLA_ROPE
Q_LORA = 512
KV_LORA = 256
MOBA_HEADS = 4
MOBA_DIM = 128
MOBA_BLOCK = 256
MOBA_TOPK = 3
MOBA_ROT = 32
CONV_WIDTH = 31
POOL_WINDOWS = (2, 4, 8, 16)
POOL_GROUP = 128
ROPE_THETA = 500000.0
NORM_EPS = 1e-5
NEG_INF = -1e30
DEPTH = 2
DEEPNORM_ALPHA = (2 * DEPTH) ** 0.25

LANES = 128
SUBLANES = 8
HEADS = MLA_HEADS + MOBA_HEADS
QK_PAD = 256
LOG2E = 1.4426950408889634
SOFTMAX_ROWS = 64
VMEM_LIMIT = 56 * 1024 * 1024

COL_CONV_A = 0
COL_CONV_G = 512
COL_CQ = 1024
COL_POOL = 1536
COL_MQ = 2048
COL_MK = 2560
COL_MV = 3072
COL_CKV = 3584
COL_KROPE = 3840
IN_PAD = 4096


def _cparams(sem):
    return pltpu.CompilerParams(dimension_semantics=sem, vmem_limit_bytes=VMEM_LIMIT)


def _matmul_kernel(x_ref, w_ref, o_ref):
    o_ref[...] = jnp.dot(x_ref[...].astype(BF16), w_ref[...],
                         preferred_element_type=F32).astype(o_ref.dtype)


def _matmul(x, w, out_dtype, tm, tn):
    m, k = x.shape
    n = w.shape[1]
    return pl.pallas_call(
        _matmul_kernel,
        grid=(m // tm, n // tn),
        in_specs=[pl.BlockSpec((tm, k), lambda i, j: (i, 0)),
                  pl.BlockSpec((k, tn), lambda i, j: (0, j))],
        out_specs=pl.BlockSpec((tm, tn), lambda i, j: (i, j)),
        out_shape=jax.ShapeDtypeStruct((m, n), out_dtype),
        compiler_params=_cparams(("parallel", "parallel")),
        name="in_proj",
    )(x, w)


def _rms(x, g):
    return x * lax.rsqrt(jnp.mean(x * x, axis=-1, keepdims=True) + NORM_EPS) * g


def _rope128(r, c, s):
    return r * c + pltpu.roll(r, 64, 1) * s


def _prep_attn_kernel(cq_ref, mq_ref, mk_ref, mv_ref, ckv_ref, kr_ref,
                      gq_ref, gkv_ref, wuq_ref, wukv_ref,
                      ca_ref, sa_ref, cm_ref, sm_ref,
                      q8_ref, k8_ref, v8_ref, kmean_ref, *, tr):
    i = pl.program_id(0)

    @pl.when(i == 0)
    def _():
        kmean_ref[...] = jnp.zeros_like(kmean_ref)

    ca = ca_ref[...]
    sa = sa_ref[...]
    cm = cm_ref[...]
    sm = sm_ref[...]
    sc_a = MLA_QK ** -0.5 * LOG2E
    sc_b = MOBA_DIM ** -0.5 * LOG2E

    qn = _rms(cq_ref[...], gq_ref[...]).astype(BF16)
    q = jnp.dot(qn, wuq_ref[...], preferred_element_type=F32)
    for h in range(MLA_HEADS):
        base = h * QK_PAD
        q8_ref[h, :, 0:LANES] = (q[:, base:base + LANES] * sc_a).astype(BF16)
        pe = _rope128(q[:, base + LANES:base + 2 * LANES], ca, sa)
        q8_ref[h, :, LANES:2 * LANES] = (pe * sc_a).astype(BF16)

    kvn = _rms(ckv_ref[...], gkv_ref[...]).astype(BF16)
    kv = jnp.dot(kvn, wukv_ref[...], preferred_element_type=F32)
    kpe = _rope128(kr_ref[...], ca, sa).astype(BF16)
    for h in range(MLA_HEADS):
        base = h * 2 * LANES
        k8_ref[h, :, 0:LANES] = kv[:, base:base + LANES].astype(BF16)
        k8_ref[h, :, LANES:2 * LANES] = kpe
        v8_ref[h] = kv[:, base + LANES:base + 2 * LANES].astype(BF16)

    row = i * tr + lax.broadcasted_iota(jnp.int32, (tr, 1), 0)
    own = lax.shift_right_logical(row, 8)
    lane = lax.broadcasted_iota(jnp.int32, (tr, LANES), 1).astype(F32)
    own = own.astype(F32)
    valid = lane < own
    is_own = lane == own
    onehot = jnp.where(is_own, 1.0, 0.0).astype(BF16)
    blocks_per_tile = tr // MOBA_BLOCK
    for h in range(MOBA_HEADS):
        sl = slice(h * LANES, (h + 1) * LANES)
        kb = _rope128(mk_ref[:, sl], cm, sm)
        qb = _rope128(mq_ref[:, sl], cm, sm)
        for b in range(blocks_per_tile):
            ksum = jnp.sum(kb[b * MOBA_BLOCK:(b + 1) * MOBA_BLOCK], axis=0, keepdims=True)
            kmean_ref[h, pl.ds(i * blocks_per_tile + b, 1), :] = ksum * (1.0 / MOBA_BLOCK)
        gate = lax.dot_general(qb, kmean_ref[h], (((1,), (1,)), ((), ())),
                               precision=lax.Precision.HIGHEST,
                               preferred_element_type=F32)
        g = jnp.where(valid, gate, NEG_INF)
        sel = jnp.zeros((tr, LANES), jnp.bool_)
        for _ in range(MOBA_TOPK):
            mx = jnp.max(g, axis=1, keepdims=True)
            idx = jnp.min(jnp.where(g == mx, lane, float(LANES)), axis=1, keepdims=True)
            hit = lane == idx
            sel = jnp.logical_or(sel, hit)
            g = jnp.where(hit, -3.0e38, g)
        ok = jnp.logical_or(jnp.logical_and(sel, valid), is_own)
        bias = jnp.where(ok, 0.0, NEG_INF)
        q8_ref[MLA_HEADS + h, :, 0:LANES] = (qb * sc_b).astype(BF16)
        q8_ref[MLA_HEADS + h, :, LANES:2 * LANES] = bias.astype(BF16)
        k8_ref[MLA_HEADS + h, :, 0:LANES] = kb.astype(BF16)
        k8_ref[MLA_HEADS + h, :, LANES:2 * LANES] = onehot
        v8_ref[MLA_HEADS + h] = mv_ref[:, sl].astype(BF16)


def _prep_attn(h, g_q, g_kv, w_uq_p, w_ukv, ca, sa, cm, sm, tr):
    s = h.shape[0]
    hb = lambda w, c: pl.BlockSpec((tr, w), lambda i, c=c: (i, c))
    full = lambda a: pl.BlockSpec(a.shape, lambda i: (0,) * a.ndim)
    tab = pl.BlockSpec((tr, LANES), lambda i: (i, 0))
    out3 = lambda w: pl.BlockSpec((HEADS, tr, w), lambda i: (0, i, 0))
    return pl.pallas_call(
        functools.partial(_prep_attn_kernel, tr=tr),
        grid=(s // tr,),
        in_specs=[hb(512, COL_CQ // 512), hb(512, COL_MQ // 512), hb(512, COL_MK // 512),
                  hb(512, COL_MV // 512), hb(256, COL_CKV // 256), hb(128, COL_KROPE // 128),
                  full(g_q), full(g_kv), full(w_uq_p), full(w_ukv), tab, tab, tab, tab],
        out_specs=[out3(QK_PAD), out3(QK_PAD), out3(LANES)],
        out_shape=[jax.ShapeDtypeStruct((HEADS, s, QK_PAD), BF16),
                   jax.ShapeDtypeStruct((HEADS, s, QK_PAD), BF16),
                   jax.ShapeDtypeStruct((HEADS, s, LANES), BF16)],
        scratch_shapes=[pltpu.VMEM((MOBA_HEADS, LANES, LANES), F32)],
        compiler_params=_cparams(("arbitrary",)),
        name="prep_attn",
    )(h, h, h, h, h, h, g_q, g_kv, w_uq_p, w_ukv, ca, sa, cm, sm)


CONV_ROWS = 64
CONV_HALO = 32
POOL_HALO = 16


def _prep_mix_kernel(a_ref, g_ref, p_ref, dww_ref, dwb_ref, lng_ref, lnb_ref, pww_ref,
                     poolw_ref, pscale_ref, o_ref, hbuf, pbuf, shift_buf, y_buf, *, tr):
    i = pl.program_id(0)

    @pl.when(i == 0)
    def _():
        hbuf[0:CONV_HALO, :] = jnp.zeros((CONV_HALO, W_GROUP), F32)
        pbuf[0:POOL_HALO, :] = jnp.zeros((POOL_HALO, W_GROUP), F32)

    @pl.when(i > 0)
    def _():
        hbuf[0:CONV_HALO, :] = hbuf[tr:tr + CONV_HALO, :]
        pbuf[0:POOL_HALO, :] = pbuf[tr:tr + POOL_HALO, :]

    hdn = a_ref[...] * jax.nn.sigmoid(g_ref[...])
    hbuf[CONV_HALO:CONV_HALO + tr, :] = hdn
    first = CONV_HALO - (CONV_WIDTH - 1)
    for c in range(W_GROUP // LANES):
        cs = slice(c * LANES, (c + 1) * LANES)
        shifted = shift_buf.at[c % 2]
        for b in range(SUBLANES):
            span = tr + CONV_WIDTH - 1 - b - (CONV_WIDTH - 1 - b) % SUBLANES
            shifted[b, 0:span, :] = hbuf[first + b:first + b + span, cs]
        for r in range(0, tr, CONV_ROWS):
            acc = jnp.zeros((CONV_ROWS, LANES), F32) + dwb_ref[:, cs]
            for j in range(CONV_WIDTH):
                b = j % SUBLANES
                acc = acc + dww_ref[j:j + 1, cs] * shifted[b, r + j - b:r + j - b + CONV_ROWS, :]
            y_buf[r:r + CONV_ROWS, cs] = acc
    y = y_buf[...]
    mu = jnp.mean(y, axis=-1, keepdims=True)
    yc = y - mu
    var = jnp.mean(yc * yc, axis=-1, keepdims=True)
    z = yc * lax.rsqrt(var + NORM_EPS) * lng_ref[...] + lnb_ref[...]
    z = z * jax.nn.sigmoid(z)
    o_ref[:, 0:W_GROUP] = jnp.dot(z.astype(BF16), pww_ref[...],
                                  preferred_element_type=F32).astype(o_ref.dtype)

    u = p_ref[...]
    pbuf[POOL_HALO:POOL_HALO + tr, :] = u
    t = i * tr + lax.broadcasted_iota(jnp.int32, (tr, 1), 0)
    for gi, w in enumerate(POOL_WINDOWS):
        sl = slice(gi * POOL_GROUP, (gi + 1) * POOL_GROUP)
        ssum = u[:, sl]
        for d in range(1, w):
            ssum = ssum + pbuf[POOL_HALO - d:POOL_HALO - d + tr, sl]
        count = jnp.minimum(t + 1, w).astype(F32)
        pooled = ssum / count - u[:, sl]
        mixed = jnp.dot(pooled.astype(BF16), poolw_ref[gi], preferred_element_type=F32)
        o_ref[:, W_GROUP + gi * POOL_GROUP:W_GROUP + (gi + 1) * POOL_GROUP] = (
            mixed * pscale_ref[:, sl]).astype(o_ref.dtype)


def _prep_mix(h, dw_w, dw_b, ln_g, ln_b, pw_w, pool_w, pool_scale, tr):
    s = h.shape[0]
    hb = lambda c: pl.BlockSpec((tr, 512), lambda i, c=c: (i, c))
    full = lambda a: pl.BlockSpec(a.shape, lambda i: (0,) * a.ndim)
    return pl.pallas_call(
        functools.partial(_prep_mix_kernel, tr=tr),
        grid=(s // tr,),
        in_specs=[hb(COL_CONV_A // 512), hb(COL_CONV_G // 512), hb(COL_POOL // 512),
                  full(dw_w), full(dw_b), full(ln_g), full(ln_b), full(pw_w),
                  full(pool_w), full(pool_scale)],
        out_specs=pl.BlockSpec((tr, 2 * W_GROUP), lambda i: (i, 0)),
        out_shape=jax.ShapeDtypeStruct((s, 2 * W_GROUP), BF16),
        scratch_shapes=[pltpu.VMEM((CONV_HALO + tr, W_GROUP), F32),
                        pltpu.VMEM((POOL_HALO + tr, W_GROUP), F32),
                        pltpu.VMEM((2, SUBLANES, CONV_HALO + tr, LANES), F32),
                        pltpu.VMEM((tr, W_GROUP), F32)],
        compiler_params=_cparams(("arbitrary",)),
        name="prep_mix",
    )(h, h, h, dw_w, dw_b, ln_g, ln_b, pw_w, pool_w, pool_scale)


def _attn_kernel(q_ref, k_ref, v_ref, o_ref, s_buf, p_buf, m_ref, l_ref, acc_ref, *, t, nh):
    i = pl.program_id(1)
    n = i + 1

    def scores(hd, j, slot):
        k = k_ref[hd, pl.ds(pl.multiple_of(j * t, t), t), :]
        s_buf[slot, hd] = lax.dot_general(q_ref[hd], k, (((1,), (1,)), ((), ())),
                                          preferred_element_type=F32)

    def softmax(hd, slot, diagonal, pv):
        for r in range(t // SOFTMAX_ROWS):
            rows = slice(r * SOFTMAX_ROWS, (r + 1) * SOFTMAX_ROWS)
            cols = [s_buf[slot, hd, rows, c * LANES:(c + 1) * LANES] for c in range(t // LANES)]
            if diagonal:
                rq = r * SOFTMAX_ROWS + lax.broadcasted_iota(jnp.int32, (SOFTMAX_ROWS, LANES), 0)
                ck = lax.broadcasted_iota(jnp.int32, (SOFTMAX_ROWS, LANES), 1)
                cols = [jnp.where(ck + c * LANES <= rq, s, NEG_INF) for c, s in enumerate(cols)]
            mx = functools.reduce(jnp.maximum, cols)
            m_old = m_ref[hd, rows]
            m_new = jnp.maximum(m_old, jnp.max(mx, axis=1, keepdims=True))
            alpha = jnp.exp2(m_old - m_new)
            ps = [jnp.exp2(s - m_new) for s in cols]
            row = functools.reduce(lambda a, b: a + b, ps)
            l_ref[hd, rows] = alpha * l_ref[hd, rows] + jnp.sum(row, axis=1, keepdims=True)
            m_ref[hd, rows] = m_new
            for c, p in enumerate(ps):
                p_buf[slot, hd, rows, c * LANES:(c + 1) * LANES] = p.astype(BF16)
            if pv is not None:
                acc_ref[hd, rows] = alpha * (acc_ref[hd, rows] + pv[rows])

    def values(hd, j, slot):
        v = v_ref[hd, pl.ds(pl.multiple_of(j * t, t), t), :]
        return jnp.dot(p_buf[slot, hd], v, preferred_element_type=F32)

    def step(j, slot, *, ahead, behind, diagonal):
        for hd in range(nh):
            pv = values(hd, j - 1, 1 - slot) if behind else None
            if ahead:
                scores(hd, j + 1, 1 - slot)
            softmax(hd, slot, diagonal, pv)

    def finish(j, slot):
        for hd in range(nh):
            acc = acc_ref[hd] + values(hd, j, slot)
            o_ref[:, hd * LANES:(hd + 1) * LANES] = (acc / l_ref[hd]).astype(o_ref.dtype)

    m_ref[...] = jnp.full(m_ref.shape, NEG_INF, F32)
    l_ref[...] = jnp.zeros(l_ref.shape, F32)
    acc_ref[...] = jnp.zeros(acc_ref.shape, F32)
    for hd in range(nh):
        scores(hd, 0, 0)

    @pl.when(n == 1)
    def _():
        step(0, 0, ahead=False, behind=False, diagonal=True)
        finish(0, 0)

    @pl.when(n > 1)
    def _():
        step(0, 0, ahead=True, behind=False, diagonal=False)

        def body(u, carry):
            j = 2 * u + 1
            step(j, 1, ahead=True, behind=True, diagonal=False)
            step(j + 1, 0, ahead=True, behind=True, diagonal=False)
            return carry

        lax.fori_loop(0, lax.div(n - 2, 2), body, 0)

        @pl.when(lax.rem(n, 2) == 1)
        def _():
            step(n - 2, 1, ahead=True, behind=True, diagonal=False)
            step(n - 1, 0, ahead=False, behind=True, diagonal=True)
            finish(n - 1, 0)

        @pl.when(lax.rem(n, 2) == 0)
        def _():
            step(n - 1, 1, ahead=False, behind=True, diagonal=True)
            finish(n - 1, 1)


def _attention(q8, k8, v8, t, nh):
    heads, s, _ = q8.shape
    resident = lambda w: pl.BlockSpec((nh, s, w), lambda h, i: (h, 0, 0),
                                      pipeline_mode=pl.Buffered(1))
    return pl.pallas_call(
        functools.partial(_attn_kernel, t=t, nh=nh),
        grid=(heads // nh, s // t),
        in_specs=[pl.BlockSpec((nh, t, QK_PAD), lambda h, i: (h, i, 0)),
                  resident(QK_PAD), resident(LANES)],
        out_specs=pl.BlockSpec((t, nh * LANES), lambda h, i: (i, h)),
        out_shape=jax.ShapeDtypeStruct((s, heads * LANES), BF16),
        scratch_shapes=[pltpu.VMEM((2, nh, t, t), F32), pltpu.VMEM((2, nh, t, t), BF16),
                        pltpu.VMEM((nh, t, LANES), F32), pltpu.VMEM((nh, t, LANES), F32),
                        pltpu.VMEM((nh, t, LANES), F32)],
        compiler_params=_cparams(("parallel", "arbitrary")),
        name="attention",
    )(q8, k8, v8)


def _layer_norm(y, g, b):
    mu = jnp.mean(y, axis=-1, keepdims=True)
    yc = y - mu
    var = jnp.mean(yc * yc, axis=-1, keepdims=True)
    return yc * lax.rsqrt(var + NORM_EPS) * g + b


def _out_proj_kernel(att_ref, mix_ref, x_ref, wa_ref, wm_ref, g_ref, b_ref, o_ref):
    y = jnp.dot(att_ref[...], wa_ref[...], preferred_element_type=F32)
    y = y + jnp.dot(mix_ref[...], wm_ref[...], preferred_element_type=F32)
    y = y + DEEPNORM_ALPHA * x_ref[...]
    o_ref[...] = _layer_norm(y, g_ref[...], b_ref[...])


def _out_proj(att, mix, x, w_att, w_mix, ln_g, ln_b, tm):
    s, d = x.shape
    row = lambda w: pl.BlockSpec((tm, w), lambda i: (i, 0))
    full = lambda a: pl.BlockSpec(a.shape, lambda i: (0,) * a.ndim)
    return pl.pallas_call(
        _out_proj_kernel,
        grid=(s // tm,),
        in_specs=[row(att.shape[1]), row(mix.shape[1]), row(d), full(w_att), full(w_mix),
                  full(ln_g), full(ln_b)],
        out_specs=row(d),
        out_shape=jax.ShapeDtypeStruct((s, d), F32),
        compiler_params=_cparams(("parallel",)),
        name="out_proj_ln",
    )(att, mix, x, w_att, w_mix, ln_g, ln_b)


def _ffn_kernel(x_ref, wg_ref, wu_ref, wd_ref, g_ref, b_ref, o_ref, xb_ref):
    f = pl.program_id(1)

    @pl.when(f == 0)
    def _():
        x = x_ref[...]
        xb_ref[...] = x.astype(BF16)
        o_ref[...] = DEEPNORM_ALPHA * x

    xb = xb_ref[...]
    gate = jnp.dot(xb, wg_ref[...], preferred_element_type=F32)
    up = jnp.dot(xb, wu_ref[...], preferred_element_type=F32)
    act = (gate * jax.nn.sigmoid(gate) * up).astype(BF16)
    o_ref[...] += jnp.dot(act, wd_ref[...], preferred_element_type=F32)

    @pl.when(f == pl.num_programs(1) - 1)
    def _():
        o_ref[...] = _layer_norm(o_ref[...], g_ref[...], b_ref[...])


def _ffn(x, w_gate, w_up, w_down, ln_g, ln_b, layer, tm, tf):
    s, d = x.shape
    dff = w_gate.shape[2]
    return pl.pallas_call(
        _ffn_kernel,
        grid=(s // tm, dff // tf),
        in_specs=[pl.BlockSpec((tm, d), lambda i, f: (i, 0)),
                  pl.BlockSpec((None, d, tf), lambda i, f: (layer, 0, f)),
                  pl.BlockSpec((None, d, tf), lambda i, f: (layer, 0, f)),
                  pl.BlockSpec((None, tf, d), lambda i, f: (layer, f, 0)),
                  pl.BlockSpec((1, d), lambda i, f: (0, 0)),
                  pl.BlockSpec((1, d), lambda i, f: (0, 0))],
        out_specs=pl.BlockSpec((tm, d), lambda i, f: (i, 0)),
        out_shape=jax.ShapeDtypeStruct((s, d), F32),
        scratch_shapes=[pltpu.VMEM((tm, d), BF16)],
        compiler_params=_cparams(("parallel", "arbitrary")),
        name="ffn_ln",
    )(x, w_gate, w_up, w_down, ln_g, ln_b)


def _moba_permute(w):
    d = w.shape[0]
    w = w.reshape(d, MOBA_HEADS, MOBA_DIM)
    half = MOBA_ROT // 2
    out = jnp.concatenate([w[:, :, :half], w[:, :, MOBA_ROT:MOBA_ROT + 48],
                           w[:, :, half:MOBA_ROT], w[:, :, MOBA_ROT + 48:]], axis=2)
    return out.reshape(d, W_GROUP)


def _relayout_w_in(w_in):
    d = w_in.shape[0]
    c_q, c_kv, k_rope, mq, mk, mv, conv_in, pool_in = jnp.split(
        w_in, [512, 768, 832, 1344, 1856, 2368, 3392], axis=1)
    permute = _moba_permute
    z32 = jnp.zeros((d, 32), w_in.dtype)
    kr = jnp.concatenate([k_rope[:, :32], z32, k_rope[:, 32:], z32], axis=1)
    pad = jnp.zeros((d, IN_PAD - COL_KROPE - LANES), w_in.dtype)
    return jnp.concatenate([conv_in[:, :512], conv_in[:, 512:], c_q, pool_in,
                            permute(mq), permute(mk), mv, c_kv, kr, pad], axis=1)


def _relayout_w_uq(w_uq):
    r = w_uq.shape[0]
    w = w_uq.reshape(r, MLA_HEADS, MLA_QK)
    z32 = jnp.zeros((r, MLA_HEADS, 32), w_uq.dtype)
    half = MLA_ROPE // 2
    out = jnp.concatenate([w[:, :, :MLA_NOPE], w[:, :, MLA_NOPE:MLA_NOPE + half], z32,
                           w[:, :, MLA_NOPE + half:], z32], axis=2)
    return out.reshape(r, MLA_HEADS * QK_PAD)


def _rope_tables(positions):
    pos = positions.astype(F32)[0][:, None]

    def cos_sin(dim):
        inv_freq = 1.0 / (ROPE_THETA ** (jnp.arange(0, dim, 2, dtype=F32) / dim))
        ang = (pos * inv_freq).reshape(-1, LANES)
        return jnp.cos(ang).reshape(-1, dim // 2), jnp.sin(ang).reshape(-1, dim // 2)

    s = pos.shape[0]
    c, sn = cos_sin(MLA_ROPE)
    z = jnp.zeros((s, 32), F32)
    ca = jnp.concatenate([c, z, c, z], axis=1)
    sa = jnp.concatenate([-sn, z, sn, z], axis=1)
    c, sn = cos_sin(MOBA_ROT)
    one = jnp.ones((s, 48), F32)
    z = jnp.zeros((s, 48), F32)
    cm = jnp.concatenate([c, one, c, one], axis=1)
    sm = jnp.concatenate([-sn, z, sn, z], axis=1)
    return ca, sa, cm, sm


def kernel(x, positions, w_in, g_q, w_uq, g_kv, w_ukv, conv_dw_w, conv_dw_b, conv_ln_g,
           conv_ln_b, conv_pw_w, pool_w, pool_scale, w_out, ln1_g, ln1_b, w_gate, w_up,
           w_down, ln2_g, ln2_b):
    b, s, d = x.shape
    assert b == 1 and d == D_MODEL and s % 1024 == 0
    ca, sa, cm, sm = _rope_tables(positions)
    row2 = lambda a: a.reshape(1, -1)
    w_gate_b, w_up_b, w_down_b = w_gate.astype(BF16), w_up.astype(BF16), w_down.astype(BF16)
    h = x[0]
    for l in range(DEPTH):
        w_in_p = _relayout_w_in(w_in[l]).astype(BF16)
        w_uq_p = _relayout_w_uq(w_uq[l]).astype(BF16)
        w_att = w_out[l][:2 * W_GROUP].astype(BF16)
        w_mix = w_out[l][2 * W_GROUP:].astype(BF16)
        dw_w = jnp.pad(conv_dw_w[l], ((0, 1), (0, 0)))

        proj = _matmul(h, w_in_p, F32, tm=1024, tn=1024)
        q8, k8, v8 = _prep_attn(proj, row2(g_q[l]), row2(g_kv[l]), w_uq_p,
                                w_ukv[l].astype(BF16), ca, sa, cm, sm, tr=256)
        mix = _prep_mix(proj, dw_w, row2(conv_dw_b[l]), row2(conv_ln_g[l]),
                        row2(conv_ln_b[l]), conv_pw_w[l].astype(BF16),
                        pool_w[l].astype(BF16), row2(pool_scale[l]), tr=256)
        att = _attention(q8, k8, v8, t=512, nh=2)
        x1 = _out_proj(att, mix, h, w_att, w_mix, row2(ln1_g[l]), row2(ln1_b[l]), tm=512)
        h = _ffn(x1, w_gate_b, w_up_b, w_down_b, row2(ln2_g[l]), row2(ln2_b[l]), layer=l,
                 tm=512, tf=512)
    return h[None]
```

```python
import functools

import jax
import jax.numpy as jnp
from jax import lax
from jax.experimental import pallas as pl
from jax.experimental.pallas import tpu as pltpu

F32 = jnp.float32
BF16 = jnp.bfloat16

D_MODEL = 2048
W_GROUP = 512
MLA_HEADS = 4
MLA_NOPE = 128
MLA_ROPE = 64
MLA_V = 128
MLA_QK = MLA_NOPE + MLA_ROPE
Q_LORA = 512
KV_LORA = 256
MOBA_HEADS = 4
MOBA_DIM = 128
MOBA_BLOCK = 256
MOBA_TOPK = 3
MOBA_ROT = 32
CONV_WIDTH = 31
POOL_WINDOWS = (2, 4, 8, 16)
POOL_GROUP = 128
ROPE_THETA = 500000.0
NORM_EPS = 1e-5
NEG_INF = -1e30
DEPTH = 2
DEEPNORM_ALPHA = (2 * DEPTH) ** 0.25

LANES = 128
SUBLANES = 8
HEADS = MLA_HEADS + MOBA_HEADS
QK_PAD = 256
LOG2E = 1.4426950408889634
SOFTMAX_ROWS = 64
VMEM_LIMIT = 56 * 1024 * 1024

COL_CONV_A = 0
COL_CONV_G = 512
COL_CQ = 1024
COL_POOL = 1536
COL_MQ = 2048
COL_MK = 2560
COL_MV = 3072
COL_CKV = 3584
COL_KROPE = 3840
IN_PAD = 4096


def _cparams(sem):
    return pltpu.CompilerParams(dimension_semantics=sem, vmem_limit_bytes=VMEM_LIMIT)


def _matmul_kernel(x_ref, w_ref, o_ref):
    o_ref[...] = jnp.dot(x_ref[...].astype(BF16), w_ref[...],
                         preferred_element_type=F32).astype(o_ref.dtype)


def _matmul(x, w, out_dtype, tm, tn):
    m, k = x.shape
    n = w.shape[1]
    return pl.pallas_call(
        _matmul_kernel,
        grid=(m // tm, n // tn),
        in_specs=[pl.BlockSpec((tm, k), lambda i, j: (i, 0)),
                  pl.BlockSpec((k, tn), lambda i, j: (0, j))],
        out_specs=pl.BlockSpec((tm, tn), lambda i, j: (i, j)),
        out_shape=jax.ShapeDtypeStruct((m, n), out_dtype),
        compiler_params=_cparams(("parallel", "parallel")),
        name="in_proj",
    )(x, w)


def _rms(x, g):
    return x * lax.rsqrt(jnp.mean(x * x, axis=-1, keepdims=True) + NORM_EPS) * g


def _rope128(r, c, s):
    return r * c + pltpu.roll(r, 64, 1) * s


def _prep_attn_kernel(cq_ref, mq_ref, mk_ref, mv_ref, ckv_ref, kr_ref,
                      gq_ref, gkv_ref, wuq_ref, wukv_ref,
                      ca_ref, sa_ref, cm_ref, sm_ref,
                      q8_ref, k8_ref, v8_ref, kmean_ref, *, tr):
    i = pl.program_id(0)

    @pl.when(i == 0)
    def _():
        kmean_ref[...] = jnp.zeros_like(kmean_ref)

    ca = ca_ref[...]
    sa = sa_ref[...]
    cm = cm_ref[...]
    sm = sm_ref[...]
    sc_a = MLA_QK ** -0.5 * LOG2E
    sc_b = MOBA_DIM ** -0.5 * LOG2E

    qn = _rms(cq_ref[...], gq_ref[...]).astype(BF16)
    q = jnp.dot(qn, wuq_ref[...], preferred_element_type=F32)
    for h in range(MLA_HEADS):
        base = h * QK_PAD
        q8_ref[h, :, 0:LANES] = (q[:, base:base + LANES] * sc_a).astype(BF16)
        pe = _rope128(q[:, base + LANES:base + 2 * LANES], ca, sa)
        q8_ref[h, :, LANES:2 * LANES] = (pe * sc_a).astype(BF16)

    kvn = _rms(ckv_ref[...], gkv_ref[...]).astype(BF16)
    kv = jnp.dot(kvn, wukv_ref[...], preferred_element_type=F32)
    kpe = _rope128(kr_ref[...], ca, sa).astype(BF16)
    for h in range(MLA_HEADS):
        base = h * 2 * LANES
        k8_ref[h, :, 0:LANES] = kv[:, base:base + LANES].astype(BF16)
        k8_ref[h, :, LANES:2 * LANES] = kpe
        v8_ref[h] = kv[:, base + LANES:base + 2 * LANES].astype(BF16)

    row = i * tr + lax.broadcasted_iota(jnp.int32, (tr, 1), 0)
    own = lax.shift_right_logical(row, 8)
    lane = lax.broadcasted_iota(jnp.int32, (tr, LANES), 1).astype(F32)
    own = own.astype(F32)
    valid = lane < own
    is_own = lane == own
    onehot = jnp.where(is_own, 1.0, 0.0).astype(BF16)
    blocks_per_tile = tr // MOBA_BLOCK
    for h in range(MOBA_HEADS):
        sl = slice(h * LANES, (h + 1) * LANES)
        kb = _rope128(mk_ref[:, sl], cm, sm)
        qb = _rope128(mq_ref[:, sl], cm, sm)
        for b in range(blocks_per_tile):
            ksum = jnp.sum(kb[b * MOBA_BLOCK:(b + 1) * MOBA_BLOCK], axis=0, keepdims=True)
            kmean_ref[h, pl.ds(i * blocks_per_tile + b, 1), :] = ksum * (1.0 / MOBA_BLOCK)
        gate = lax.dot_general(qb, kmean_ref[h], (((1,), (1,)), ((), ())),
                               precision=lax.Precision.HIGHEST,
                               preferred_element_type=F32)
        g = jnp.where(valid, gate, NEG_INF)
        sel = jnp.zeros((tr, LANES), jnp.bool_)
        for _ in range(MOBA_TOPK):
            mx = jnp.max(g, axis=1, keepdims=True)
            idx = jnp.min(jnp.where(g == mx, lane, float(LANES)), axis=1, keepdims=True)
            hit = lane == idx
            sel = jnp.logical_or(sel, hit)
            g = jnp.where(hit, -3.0e38, g)
        ok = jnp.logical_or(jnp.logical_and(sel, valid), is_own)
        bias = jnp.where(ok, 0.0, NEG_INF)
        q8_ref[MLA_HEADS + h, :, 0:LANES] = (qb * sc_b).astype(BF16)
        q8_ref[MLA_HEADS + h, :, LANES:2 * LANES] = bias.astype(BF16)
        k8_ref[MLA_HEADS + h, :, 0:LANES] = kb.astype(BF16)
        k8_ref[MLA_HEADS + h, :, LANES:2 * LANES] = onehot
        v8_ref[MLA_HEADS + h] = mv_ref[:, sl].astype(BF16)


def _prep_attn(h, g_q, g_kv, w_uq_p, w_ukv, ca, sa, cm, sm, tr):
    s = h.shape[0]
    hb = lambda w, c: pl.BlockSpec((tr, w), lambda i, c=c: (i, c))
    full = lambda a: pl.BlockSpec(a.shape, lambda i: (0,) * a.ndim)
    tab = pl.BlockSpec((tr, LANES), lambda i: (i, 0))
    out3 = lambda w: pl.BlockSpec((HEADS, tr, w), lambda i: (0, i, 0))
    return pl.pallas_call(
        functools.partial(_prep_attn_kernel, tr=tr),
        grid=(s // tr,),
        in_specs=[hb(512, COL_CQ // 512), hb(512, COL_MQ // 512), hb(512, COL_MK // 512),
                  hb(512, COL_MV // 512), hb(256, COL_CKV // 256), hb(128, COL_KROPE // 128),
                  full(g_q), full(g_kv), full(w_uq_p), full(w_ukv), tab, tab, tab, tab],
        out_specs=[out3(QK_PAD), out3(QK_PAD), out3(LANES)],
        out_shape=[jax.ShapeDtypeStruct((HEADS, s, QK_PAD), BF16),
                   jax.ShapeDtypeStruct((HEADS, s, QK_PAD), BF16),
                   jax.ShapeDtypeStruct((HEADS, s, LANES), BF16)],
        scratch_shapes=[pltpu.VMEM((MOBA_HEADS, LANES, LANES), F32)],
        compiler_params=_cparams(("arbitrary",)),
        name="prep_attn",
    )(h, h, h, h, h, h, g_q, g_kv, w_uq_p, w_ukv, ca, sa, cm, sm)


CONV_ROWS = 64
CONV_HALO = 32
POOL_HALO = 16


def _prep_mix_kernel(a_ref, g_ref, p_ref, dww_ref, dwb_ref, lng_ref, lnb_ref, pww_ref,
                     poolw_ref, pscale_ref, o_ref, hbuf, pbuf, shift_buf, y_buf, *, tr):
    i = pl.program_id(0)

    @pl.when(i == 0)
    def _():
        hbuf[0:CONV_HALO, :] = jnp.zeros((CONV_HALO, W_GROUP), F32)
        pbuf[0:POOL_HALO, :] = jnp.zeros((POOL_HALO, W_GROUP), F32)

    @pl.when(i > 0)
    def _():
        hbuf[0:CONV_HALO, :] = hbuf[tr:tr + CONV_HALO, :]
        pbuf[0:POOL_HALO, :] = pbuf[tr:tr + POOL_HALO, :]

    hdn = a_ref[...] * jax.nn.sigmoid(g_ref[...])
    hbuf[CONV_HALO:CONV_HALO + tr, :] = hdn
    first = CONV_HALO - (CONV_WIDTH - 1)
    for c in range(W_GROUP // LANES):
        cs = slice(c * LANES, (c + 1) * LANES)
        shifted = shift_buf.at[c % 2]
        for b in range(SUBLANES):
            span = tr + CONV_WIDTH - 1 - b - (CONV_WIDTH - 1 - b) % SUBLANES
            shifted[b, 0:span, :] = hbuf[first + b:first + b + span, cs]
        for r in range(0, tr, CONV_ROWS):
            acc = jnp.zeros((CONV_ROWS, LANES), F32) + dwb_ref[:, cs]
            for j in range(CONV_WIDTH):
                b = j % SUBLANES
                acc = acc + dww_ref[j:j + 1, cs] * shifted[b, r + j - b:r + j - b + CONV_ROWS, :]
            y_buf[r:r + CONV_ROWS, cs] = acc
    y = y_buf[...]
    mu = jnp.mean(y, axis=-1, keepdims=True)
    yc = y - mu
    var = jnp.mean(yc * yc, axis=-1, keepdims=True)
    z = yc * lax.rsqrt(var + NORM_EPS) * lng_ref[...] + lnb_ref[...]
    z = z * jax.nn.sigmoid(z)
    o_ref[:, 0:W_GROUP] = jnp.dot(z.astype(BF16), pww_ref[...],
                                  preferred_element_type=F32).astype(o_ref.dtype)

    u = p_ref[...]
    pbuf[POOL_HALO:POOL_HALO + tr, :] = u
    t = i * tr + lax.broadcasted_iota(jnp.int32, (tr, 1), 0)
    for gi, w in enumerate(POOL_WINDOWS):
        sl = slice(gi * POOL_GROUP, (gi + 1) * POOL_GROUP)
        ssum = u[:, sl]
        for d in range(1, w):
            ssum = ssum + pbuf[POOL_HALO - d:POOL_HALO - d + tr, sl]
        count = jnp.minimum(t + 1, w).astype(F32)
        pooled = ssum / count - u[:, sl]
        mixed = jnp.dot(pooled.astype(BF16), poolw_ref[gi], preferred_element_type=F32)
        o_ref[:, W_GROUP + gi * POOL_GROUP:W_GROUP + (gi + 1) * POOL_GROUP] = (
            mixed * pscale_ref[:, sl]).astype(o_ref.dtype)


def _prep_mix(h, dw_w, dw_b, ln_g, ln_b, pw_w, pool_w, pool_scale, tr):
    s = h.shape[0]
    hb = lambda c: pl.BlockSpec((tr, 512), lambda i, c=c: (i, c))
    full = lambda a: pl.BlockSpec(a.shape, lambda i: (0,) * a.ndim)
    return pl.pallas_call(
        functools.partial(_prep_mix_kernel, tr=tr),
        grid=(s // tr,),
        in_specs=[hb(COL_CONV_A // 512), hb(COL_CONV_G // 512), hb(COL_POOL // 512),
                  full(dw_w), full(dw_b), full(ln_g), full(ln_b), full(pw_w),
                  full(pool_w), full(pool_scale)],
        out_specs=pl.BlockSpec((tr, 2 * W_GROUP), lambda i: (i, 0)),
        out_shape=jax.ShapeDtypeStruct((s, 2 * W_GROUP), BF16),
        scratch_shapes=[pltpu.VMEM((CONV_HALO + tr, W_GROUP), F32),
                        pltpu.VMEM((POOL_HALO + tr, W_GROUP), F32),
                        pltpu.VMEM((2, SUBLANES, CONV_HALO + tr, LANES), F32),
                        pltpu.VMEM((tr, W_GROUP), F32)],
        compiler_params=_cparams(("arbitrary",)),
        name="prep_mix",
    )(h, h, h, dw_w, dw_b, ln_g, ln_b, pw_w, pool_w, pool_scale)


def _attn_kernel(q_ref, k_ref, v_ref, o_ref, s_buf, p_buf, m_ref, l_ref, acc_ref, *, t, nh):
    i = pl.program_id(1)
    n = i + 1

    def scores(hd, j, slot):
        k = k_ref[hd, pl.ds(pl.multiple_of(j * t, t), t), :]
        s_buf[slot, hd] = lax.dot_general(q_ref[hd], k, (((1,), (1,)), ((), ())),
                                          preferred_element_type=F32)

    def softmax(hd, slot, diagonal, pv):
        for r in range(t // SOFTMAX_ROWS):
            rows = slice(r * SOFTMAX_ROWS, (r + 1) * SOFTMAX_ROWS)
            cols = [s_buf[slot, hd, rows, c * LANES:(c + 1) * LANES] for c in range(t // LANES)]
            if diagonal:
                rq = r * SOFTMAX_ROWS + lax.broadcasted_iota(jnp.int32, (SOFTMAX_ROWS, LANES), 0)
                ck = lax.broadcasted_iota(jnp.int32, (SOFTMAX_ROWS, LANES), 1)
                cols = [jnp.where(ck + c * LANES <= rq, s, NEG_INF) for c, s in enumerate(cols)]
            mx = functools.reduce(jnp.maximum, cols)
            m_old = m_ref[hd, rows]
            m_new = jnp.maximum(m_old, jnp.max(mx, axis=1, keepdims=True))
            alpha = jnp.exp2(m_old - m_new)
            ps = [jnp.exp2(s - m_new) for s in cols]
            row = functools.reduce(lambda a, b: a + b, ps)
            l_ref[hd, rows] = alpha * l_ref[hd, rows] + jnp.sum(row, axis=1, keepdims=True)
            m_ref[hd, rows] = m_new
            for c, p in enumerate(ps):
                p_buf[slot, hd, rows, c * LANES:(c + 1) * LANES] = p.astype(BF16)
            if pv is not None:
                acc_ref[hd, rows] = alpha * (acc_ref[hd, rows] + pv[rows])

    def values(hd, j, slot):
        v = v_ref[hd, pl.ds(pl.multiple_of(j * t, t), t), :]
        return jnp.dot(p_buf[slot, hd], v, preferred_element_type=F32)

    def step(j, slot, *, ahead, behind, diagonal):
        for hd in range(nh):
            pv = values(hd, j - 1, 1 - slot) if behind else None
            if ahead:
                scores(hd, j + 1, 1 - slot)
            softmax(hd, slot, diagonal, pv)

    def finish(j, slot):
        for hd in range(nh):
            acc = acc_ref[hd] + values(hd, j, slot)
            o_ref[:, hd * LANES:(hd + 1) * LANES] = (acc / l_ref[hd]).astype(o_ref.dtype)

    m_ref[...] = jnp.full(m_ref.shape, NEG_INF, F32)
    l_ref[...] = jnp.zeros(l_ref.shape, F32)
    acc_ref[...] = jnp.zeros(acc_ref.shape, F32)
    for hd in range(nh):
        scores(hd, 0, 0)

    @pl.when(n == 1)
    def _():
        step(0, 0, ahead=False, behind=False, diagonal=True)
        finish(0, 0)

    @pl.when(n > 1)
    def _():
        step(0, 0, ahead=True, behind=False, diagonal=False)

        def pair(j):
            step(j, 1, ahead=True, behind=True, diagonal=False)
            step(j + 1, 0, ahead=True, behind=True, diagonal=False)

        def body(u, carry):
            pair(4 * u + 1)
            pair(4 * u + 3)
            return carry

        quads = lax.div(n - 2, 4)
        lax.fori_loop(0, quads, body, 0)

        @pl.when(lax.rem(n - 2, 4) >= 2)
        def _():
            pair(4 * quads + 1)

        @pl.when(lax.rem(n, 2) == 1)
        def _():
            step(n - 2, 1, ahead=True, behind=True, diagonal=False)
            step(n - 1, 0, ahead=False, behind=True, diagonal=True)
            finish(n - 1, 0)

        @pl.when(lax.rem(n, 2) == 0)
        def _():
            step(n - 1, 1, ahead=False, behind=True, diagonal=True)
            finish(n - 1, 1)


def _attention(q8, k8, v8, t, nh):
    heads, s, _ = q8.shape
    resident = lambda w: pl.BlockSpec((nh, s, w), lambda h, i: (h, 0, 0),
                                      pipeline_mode=pl.Buffered(1))
    return pl.pallas_call(
        functools.partial(_attn_kernel, t=t, nh=nh),
        grid=(heads // nh, s // t),
        in_specs=[pl.BlockSpec((nh, t, QK_PAD), lambda h, i: (h, i, 0)),
                  resident(QK_PAD), resident(LANES)],
        out_specs=pl.BlockSpec((t, nh * LANES), lambda h, i: (i, h)),
        out_shape=jax.ShapeDtypeStruct((s, heads * LANES), BF16),
        scratch_shapes=[pltpu.VMEM((2, nh, t, t), F32), pltpu.VMEM((2, nh, t, t), BF16),
                        pltpu.VMEM((nh, t, LANES), F32), pltpu.VMEM((nh, t, LANES), F32),
                        pltpu.VMEM((nh, t, LANES), F32)],
        compiler_params=_cparams(("parallel", "arbitrary")),
        name="attention",
    )(q8, k8, v8)


def _layer_norm(y, g, b):
    mu = jnp.mean(y, axis=-1, keepdims=True)
    yc = y - mu
    var = jnp.mean(yc * yc, axis=-1, keepdims=True)
    return yc * lax.rsqrt(var + NORM_EPS) * g + b


def _out_proj_kernel(att_ref, mix_ref, x_ref, wa_ref, wm_ref, g_ref, b_ref, o_ref):
    y = jnp.dot(att_ref[...], wa_ref[...], preferred_element_type=F32)
    y = y + jnp.dot(mix_ref[...], wm_ref[...], preferred_element_type=F32)
    y = y + DEEPNORM_ALPHA * x_ref[...]
    o_ref[...] = _layer_norm(y, g_ref[...], b_ref[...])


def _out_proj(att, mix, x, w_att, w_mix, ln_g, ln_b, tm):
    s, d = x.shape
    row = lambda w: pl.BlockSpec((tm, w), lambda i: (i, 0))
    full = lambda a: pl.BlockSpec(a.shape, lambda i: (0,) * a.ndim)
    return pl.pallas_call(
        _out_proj_kernel,
        grid=(s // tm,),
        in_specs=[row(att.shape[1]), row(mix.shape[1]), row(d), full(w_att), full(w_mix),
                  full(ln_g), full(ln_b)],
        out_specs=row(d),
        out_shape=jax.ShapeDtypeStruct((s, d), F32),
        compiler_params=_cparams(("parallel",)),
        name="out_proj_ln",
    )(att, mix, x, w_att, w_mix, ln_g, ln_b)


def _ffn_kernel(x_ref, wg_ref, wu_ref, wd_ref, g_ref, b_ref, o_ref, xb_ref):
    f = pl.program_id(1)

    @pl.when(f == 0)
    def _():
        x = x_ref[...]
        xb_ref[...] = x.astype(BF16)
        o_ref[...] = DEEPNORM_ALPHA * x

    xb = xb_ref[...]
    gate = jnp.dot(xb, wg_ref[...], preferred_element_type=F32)
    up = jnp.dot(xb, wu_ref[...], preferred_element_type=F32)
    act = (gate * jax.nn.sigmoid(gate) * up).astype(BF16)
    o_ref[...] += jnp.dot(act, wd_ref[...], preferred_element_type=F32)

    @pl.when(f == pl.num_programs(1) - 1)
    def _():
        o_ref[...] = _layer_norm(o_ref[...], g_ref[...], b_ref[...])


def _ffn(x, w_gate, w_up, w_down, ln_g, ln_b, layer, tm, tf):
    s, d = x.shape
    dff = w_gate.shape[2]
    return pl.pallas_call(
        _ffn_kernel,
        grid=(s // tm, dff // tf),
        in_specs=[pl.BlockSpec((tm, d), lambda i, f: (i, 0)),
                  pl.BlockSpec((None, d, tf), lambda i, f: (layer, 0, f)),
                  pl.BlockSpec((None, d, tf), lambda i, f: (layer, 0, f)),
                  pl.BlockSpec((None, tf, d), lambda i, f: (layer, f, 0)),
                  pl.BlockSpec((1, d), lambda i, f: (0, 0)),
                  pl.BlockSpec((1, d), lambda i, f: (0, 0))],
        out_specs=pl.BlockSpec((tm, d), lambda i, f: (i, 0)),
        out_shape=jax.ShapeDtypeStruct((s, d), F32),
        scratch_shapes=[pltpu.VMEM((tm, d), BF16)],
        compiler_params=_cparams(("parallel", "arbitrary")),
        name="ffn_ln",
    )(x, w_gate, w_up, w_down, ln_g, ln_b)


def _moba_permute(w):
    d = w.shape[0]
    w = w.reshape(d, MOBA_HEADS, MOBA_DIM)
    half = MOBA_ROT // 2
    out = jnp.concatenate([w[:, :, :half], w[:, :, MOBA_ROT:MOBA_ROT + 48],
                           w[:, :, half:MOBA_ROT], w[:, :, MOBA_ROT + 48:]], axis=2)
    return out.reshape(d, W_GROUP)


def _relayout_w_in(w_in):
    d = w_in.shape[0]
    c_q, c_kv, k_rope, mq, mk, mv, conv_in, pool_in = jnp.split(
        w_in, [512, 768, 832, 1344, 1856, 2368, 3392], axis=1)
    permute = _moba_permute
    z32 = jnp.zeros((d, 32), w_in.dtype)
    kr = jnp.concatenate([k_rope[:, :32], z32, k_rope[:, 32:], z32], axis=1)
    pad = jnp.zeros((d, IN_PAD - COL_KROPE - LANES), w_in.dtype)
    return jnp.concatenate([conv_in[:, :512], conv_in[:, 512:], c_q, pool_in,
                            permute(mq), permute(mk), mv, c_kv, kr, pad], axis=1)


def _relayout_w_uq(w_uq):
    r = w_uq.shape[0]
    w = w_uq.reshape(r, MLA_HEADS, MLA_QK)
    z32 = jnp.zeros((r, MLA_HEADS, 32), w_uq.dtype)
    half = MLA_ROPE // 2
    out = jnp.concatenate([w[:, :, :MLA_NOPE], w[:, :, MLA_NOPE:MLA_NOPE + half], z32,
                           w[:, :, MLA_NOPE + half:], z32], axis=2)
    return out.reshape(r, MLA_HEADS * QK_PAD)


def _rope_tables(positions):
    pos = positions.astype(F32)[0][:, None]

    def cos_sin(dim):
        inv_freq = 1.0 / (ROPE_THETA ** (jnp.arange(0, dim, 2, dtype=F32) / dim))
        ang = (pos * inv_freq).reshape(-1, LANES)
        return jnp.cos(ang).reshape(-1, dim // 2), jnp.sin(ang).reshape(-1, dim // 2)

    s = pos.shape[0]
    c, sn = cos_sin(MLA_ROPE)
    z = jnp.zeros((s, 32), F32)
    ca = jnp.concatenate([c, z, c, z], axis=1)
    sa = jnp.concatenate([-sn, z, sn, z], axis=1)
    c, sn = cos_sin(MOBA_ROT)
    one = jnp.ones((s, 48), F32)
    z = jnp.zeros((s, 48), F32)
    cm = jnp.concatenate([c, one, c, one], axis=1)
    sm = jnp.concatenate([-sn, z, sn, z], axis=1)
    return ca, sa, cm, sm


def kernel(x, positions, w_in, g_q, w_uq, g_kv, w_ukv, conv_dw_w, conv_dw_b, conv_ln_g,
           conv_ln_b, conv_pw_w, pool_w, pool_scale, w_out, ln1_g, ln1_b, w_gate, w_up,
           w_down, ln2_g, ln2_b):
    b, s, d = x.shape
    assert b == 1 and d == D_MODEL and s % 1024 == 0
    ca, sa, cm, sm = _rope_tables(positions)
    row2 = lambda a: a.reshape(1, -1)
    w_gate_b, w_up_b, w_down_b = w_gate.astype(BF16), w_up.astype(BF16), w_down.astype(BF16)
    h = x[0]
    for l in range(DEPTH):
        w_in_p = _relayout_w_in(w_in[l]).astype(BF16)
        w_uq_p = _relayout_w_uq(w_uq[l]).astype(BF16)
        w_att = w_out[l][:2 * W_GROUP].astype(BF16)
        w_mix = w_out[l][2 * W_GROUP:].astype(BF16)
        dw_w = jnp.pad(conv_dw_w[l], ((0, 1), (0, 0)))

        proj = _matmul(h, w_in_p, F32, tm=1024, tn=1024)
        q8, k8, v8 = _prep_attn(proj, row2(g_q[l]), row2(g_kv[l]), w_uq_p,
                                w_ukv[l].astype(BF16), ca, sa, cm, sm, tr=256)
        mix = _prep_mix(proj, dw_w, row2(conv_dw_b[l]), row2(conv_ln_g[l]),
                        row2(conv_ln_b[l]), conv_pw_w[l].astype(BF16),
                        pool_w[l].astype(BF16), row2(pool_scale[l]), tr=256)
        att = _attention(q8, k8, v8, t=512, nh=2)
        x1 = _out_proj(att, mix, h, w_att, w_mix, row2(ln1_g[l]), row2(ln1_b[l]), tm=512)
        h = _ffn(x1, w_gate_b, w_up_b, w_down_b, row2(ln2_g[l]), row2(ln2_b[l]), layer=l,
                 tm=512, tf=512)
    return h[None]
```

```python
import functools

import jax
import jax.numpy as jnp
from jax import lax
from jax.experimental import pallas as pl
from jax.experimental.pallas import tpu as pltpu

F32 = jnp.float32
BF16 = jnp.bfloat16

D_MODEL = 2048
W_GROUP = 512
MLA_HEADS = 4
MLA_NOPE = 128
MLA_ROPE = 64
MLA_V = 128
MLA_QK = MLA_NOPE + MLA_ROPE
Q_LORA = 512
KV_LORA = 256
MOBA_HEADS = 4
MOBA_DIM = 128
MOBA_BLOCK = 256
MOBA_TOPK = 3
MOBA_ROT = 32
CONV_WIDTH = 31
POOL_WINDOWS = (2, 4, 8, 16)
POOL_GROUP = 128
ROPE_THETA = 500000.0
NORM_EPS = 1e-5
NEG_INF = -1e30
DEPTH = 2
DEEPNORM_ALPHA = (2 * DEPTH) ** 0.25

LANES = 128
SUBLANES = 8
HEADS = MLA_HEADS + MOBA_HEADS
QK_PAD = 256
LOG2E = 1.4426950408889634
ATTN_PAIRS = 3
SOFTMAX_ROWS = 64
VMEM_LIMIT = 56 * 1024 * 1024

COL_CONV_A = 0
COL_CONV_G = 512
COL_CQ = 1024
COL_POOL = 1536
COL_MQ = 2048
COL_MK = 2560
COL_MV = 3072
COL_CKV = 3584
COL_KROPE = 3840
IN_PAD = 4096


def _cparams(sem):
    return pltpu.CompilerParams(dimension_semantics=sem, vmem_limit_bytes=VMEM_LIMIT)


def _matmul_kernel(x_ref, w_ref, o_ref):
    o_ref[...] = jnp.dot(x_ref[...].astype(BF16), w_ref[...],
                         preferred_element_type=F32).astype(o_ref.dtype)


def _matmul(x, w, out_dtype, tm, tn):
    m, k = x.shape
    n = w.shape[1]
    return pl.pallas_call(
        _matmul_kernel,
        grid=(m // tm, n // tn),
        in_specs=[pl.BlockSpec((tm, k), lambda i, j: (i, 0)),
                  pl.BlockSpec((k, tn), lambda i, j: (0, j))],
        out_specs=pl.BlockSpec((tm, tn), lambda i, j: (i, j)),
        out_shape=jax.ShapeDtypeStruct((m, n), out_dtype),
        compiler_params=_cparams(("parallel", "parallel")),
        name="in_proj",
    )(x, w)


def _rms(x, g):
    return x * lax.rsqrt(jnp.mean(x * x, axis=-1, keepdims=True) + NORM_EPS) * g


def _rope128(r, c, s):
    return r * c + pltpu.roll(r, 64, 1) * s


def _prep_attn_kernel(cq_ref, mq_ref, mk_ref, mv_ref, ckv_ref, kr_ref,
                      gq_ref, gkv_ref, wuq_ref, wukv_ref,
                      ca_ref, sa_ref, cm_ref, sm_ref,
                      q8_ref, k8_ref, v8_ref, kmean_ref, *, tr):
    i = pl.program_id(0)

    @pl.when(i == 0)
    def _():
        kmean_ref[...] = jnp.zeros_like(kmean_ref)

    ca = ca_ref[...]
    sa = sa_ref[...]
    cm = cm_ref[...]
    sm = sm_ref[...]
    sc_a = MLA_QK ** -0.5 * LOG2E
    sc_b = MOBA_DIM ** -0.5 * LOG2E

    qn = _rms(cq_ref[...], gq_ref[...]).astype(BF16)
    q = jnp.dot(qn, wuq_ref[...], preferred_element_type=F32)
    for h in range(MLA_HEADS):
        base = h * QK_PAD
        q8_ref[h, :, 0:LANES] = (q[:, base:base + LANES] * sc_a).astype(BF16)
        pe = _rope128(q[:, base + LANES:base + 2 * LANES], ca, sa)
        q8_ref[h, :, LANES:2 * LANES] = (pe * sc_a).astype(BF16)

    kvn = _rms(ckv_ref[...], gkv_ref[...]).astype(BF16)
    kv = jnp.dot(kvn, wukv_ref[...], preferred_element_type=F32)
    kpe = _rope128(kr_ref[...], ca, sa).astype(BF16)
    for h in range(MLA_HEADS):
        base = h * 2 * LANES
        k8_ref[h, :, 0:LANES] = kv[:, base:base + LANES].astype(BF16)
        k8_ref[h, :, LANES:2 * LANES] = kpe
        v8_ref[h] = kv[:, base + LANES:base + 2 * LANES].astype(BF16)

    row = i * tr + lax.broadcasted_iota(jnp.int32, (tr, 1), 0)
    own = lax.shift_right_logical(row, 8)
    lane = lax.broadcasted_iota(jnp.int32, (tr, LANES), 1).astype(F32)
    own = own.astype(F32)
    valid = lane < own
    is_own = lane == own
    onehot = jnp.where(is_own, 1.0, 0.0).astype(BF16)
    blocks_per_tile = tr // MOBA_BLOCK
    for h in range(MOBA_HEADS):
        sl = slice(h * LANES, (h + 1) * LANES)
        kb = _rope128(mk_ref[:, sl], cm, sm)
        qb = _rope128(mq_ref[:, sl], cm, sm)
        for b in range(blocks_per_tile):
            ksum = jnp.sum(kb[b * MOBA_BLOCK:(b + 1) * MOBA_BLOCK], axis=0, keepdims=True)
            kmean_ref[h, pl.ds(i * blocks_per_tile + b, 1), :] = ksum * (1.0 / MOBA_BLOCK)
        gate = lax.dot_general(qb, kmean_ref[h], (((1,), (1,)), ((), ())),
                               precision=lax.Precision.HIGHEST,
                               preferred_element_type=F32)
        g = jnp.where(valid, gate, NEG_INF)
        sel = jnp.zeros((tr, LANES), jnp.bool_)
        for _ in range(MOBA_TOPK):
            mx = jnp.max(g, axis=1, keepdims=True)
            idx = jnp.min(jnp.where(g == mx, lane, float(LANES)), axis=1, keepdims=True)
            hit = lane == idx
            sel = jnp.logical_or(sel, hit)
            g = jnp.where(hit, -3.0e38, g)
        ok = jnp.logical_or(jnp.logical_and(sel, valid), is_own)
        bias = jnp.where(ok, 0.0, NEG_INF)
        q8_ref[MLA_HEADS + h, :, 0:LANES] = (qb * sc_b).astype(BF16)
        q8_ref[MLA_HEADS + h, :, LANES:2 * LANES] = bias.astype(BF16)
        k8_ref[MLA_HEADS + h, :, 0:LANES] = kb.astype(BF16)
        k8_ref[MLA_HEADS + h, :, LANES:2 * LANES] = onehot
        v8_ref[MLA_HEADS + h] = mv_ref[:, sl].astype(BF16)


def _prep_attn(h, g_q, g_kv, w_uq_p, w_ukv, ca, sa, cm, sm, tr):
    s = h.shape[0]
    hb = lambda w, c: pl.BlockSpec((tr, w), lambda i, c=c: (i, c))
    full = lambda a: pl.BlockSpec(a.shape, lambda i: (0,) * a.ndim)
    tab = pl.BlockSpec((tr, LANES), lambda i: (i, 0))
    out3 = lambda w: pl.BlockSpec((HEADS, tr, w), lambda i: (0, i, 0))
    return pl.pallas_call(
        functools.partial(_prep_attn_kernel, tr=tr),
        grid=(s // tr,),
        in_specs=[hb(512, COL_CQ // 512), hb(512, COL_MQ // 512), hb(512, COL_MK // 512),
                  hb(512, COL_MV // 512), hb(256, COL_CKV // 256), hb(128, COL_KROPE // 128),
                  full(g_q), full(g_kv), full(w_uq_p), full(w_ukv), tab, tab, tab, tab],
        out_specs=[out3(QK_PAD), out3(QK_PAD), out3(LANES)],
        out_shape=[jax.ShapeDtypeStruct((HEADS, s, QK_PAD), BF16),
                   jax.ShapeDtypeStruct((HEADS, s, QK_PAD), BF16),
                   jax.ShapeDtypeStruct((HEADS, s, LANES), BF16)],
        scratch_shapes=[pltpu.VMEM((MOBA_HEADS, LANES, LANES), F32)],
        compiler_params=_cparams(("arbitrary",)),
        name="prep_attn",
    )(h, h, h, h, h, h, g_q, g_kv, w_uq_p, w_ukv, ca, sa, cm, sm)


CONV_ROWS = 64
CONV_HALO = 32
POOL_HALO = 16


def _prep_mix_kernel(a_ref, g_ref, p_ref, dww_ref, dwb_ref, lng_ref, lnb_ref, pww_ref,
                     poolw_ref, pscale_ref, o_ref, hbuf, pbuf, shift_buf, y_buf, *, tr):
    i = pl.program_id(0)

    @pl.when(i == 0)
    def _():
        hbuf[0:CONV_HALO, :] = jnp.zeros((CONV_HALO, W_GROUP), F32)
        pbuf[0:POOL_HALO, :] = jnp.zeros((POOL_HALO, W_GROUP), F32)

    @pl.when(i > 0)
    def _():
        hbuf[0:CONV_HALO, :] = hbuf[tr:tr + CONV_HALO, :]
        pbuf[0:POOL_HALO, :] = pbuf[tr:tr + POOL_HALO, :]

    hdn = a_ref[...] * jax.nn.sigmoid(g_ref[...])
    hbuf[CONV_HALO:CONV_HALO + tr, :] = hdn
    first = CONV_HALO - (CONV_WIDTH - 1)
    for c in range(W_GROUP // LANES):
        cs = slice(c * LANES, (c + 1) * LANES)
        shifted = shift_buf.at[c % 2]
        for b in range(SUBLANES):
            span = tr + CONV_WIDTH - 1 - b - (CONV_WIDTH - 1 - b) % SUBLANES
            shifted[b, 0:span, :] = hbuf[first + b:first + b + span, cs]
        for r in range(0, tr, CONV_ROWS):
            acc = jnp.zeros((CONV_ROWS, LANES), F32) + dwb_ref[:, cs]
            for j in range(CONV_WIDTH):
                b = j % SUBLANES
                acc = acc + dww_ref[j:j + 1, cs] * shifted[b, r + j - b:r + j - b + CONV_ROWS, :]
            y_buf[r:r + CONV_ROWS, cs] = acc
    y = y_buf[...]
    mu = jnp.mean(y, axis=-1, keepdims=True)
    yc = y - mu
    var = jnp.mean(yc * yc, axis=-1, keepdims=True)
    z = yc * lax.rsqrt(var + NORM_EPS) * lng_ref[...] + lnb_ref[...]
    z = z * jax.nn.sigmoid(z)
    o_ref[:, 0:W_GROUP] = jnp.dot(z.astype(BF16), pww_ref[...],
                                  preferred_element_type=F32).astype(o_ref.dtype)

    u = p_ref[...]
    pbuf[POOL_HALO:POOL_HALO + tr, :] = u
    t = i * tr + lax.broadcasted_iota(jnp.int32, (tr, 1), 0)
    for gi, w in enumerate(POOL_WINDOWS):
        sl = slice(gi * POOL_GROUP, (gi + 1) * POOL_GROUP)
        ssum = u[:, sl]
        for d in range(1, w):
            ssum = ssum + pbuf[POOL_HALO - d:POOL_HALO - d + tr, sl]
        count = jnp.minimum(t + 1, w).astype(F32)
        pooled = ssum / count - u[:, sl]
        mixed = jnp.dot(pooled.astype(BF16), poolw_ref[gi], preferred_element_type=F32)
        o_ref[:, W_GROUP + gi * POOL_GROUP:W_GROUP + (gi + 1) * POOL_GROUP] = (
            mixed * pscale_ref[:, sl]).astype(o_ref.dtype)


def _prep_mix(h, dw_w, dw_b, ln_g, ln_b, pw_w, pool_w, pool_scale, tr):
    s = h.shape[0]
    hb = lambda c: pl.BlockSpec((tr, 512), lambda i, c=c: (i, c))
    full = lambda a: pl.BlockSpec(a.shape, lambda i: (0,) * a.ndim)
    return pl.pallas_call(
        functools.partial(_prep_mix_kernel, tr=tr),
        grid=(s // tr,),
        in_specs=[hb(COL_CONV_A // 512), hb(COL_CONV_G // 512), hb(COL_POOL // 512),
                  full(dw_w), full(dw_b), full(ln_g), full(ln_b), full(pw_w),
                  full(pool_w), full(pool_scale)],
        out_specs=pl.BlockSpec((tr, 2 * W_GROUP), lambda i: (i, 0)),
        out_shape=jax.ShapeDtypeStruct((s, 2 * W_GROUP), BF16),
        scratch_shapes=[pltpu.VMEM((CONV_HALO + tr, W_GROUP), F32),
                        pltpu.VMEM((POOL_HALO + tr, W_GROUP), F32),
                        pltpu.VMEM((2, SUBLANES, CONV_HALO + tr, LANES), F32),
                        pltpu.VMEM((tr, W_GROUP), F32)],
        compiler_params=_cparams(("arbitrary",)),
        name="prep_mix",
    )(h, h, h, dw_w, dw_b, ln_g, ln_b, pw_w, pool_w, pool_scale)


def _attn_kernel(q_ref, k_ref, v_ref, o_ref, s_buf, p_buf, m_ref, l_ref, acc_ref, *, t, nh):
    i = pl.program_id(1)
    n = i + 1

    def scores(hd, j, slot):
        k = k_ref[hd, pl.ds(pl.multiple_of(j * t, t), t), :]
        s_buf[slot, hd] = lax.dot_general(q_ref[hd], k, (((1,), (1,)), ((), ())),
                                          preferred_element_type=F32)

    def softmax(hd, slot, diagonal, pv):
        for r in range(t // SOFTMAX_ROWS):
            rows = slice(r * SOFTMAX_ROWS, (r + 1) * SOFTMAX_ROWS)
            cols = [s_buf[slot, hd, rows, c * LANES:(c + 1) * LANES] for c in range(t // LANES)]
            if diagonal:
                rq = r * SOFTMAX_ROWS + lax.broadcasted_iota(jnp.int32, (SOFTMAX_ROWS, LANES), 0)
                ck = lax.broadcasted_iota(jnp.int32, (SOFTMAX_ROWS, LANES), 1)
                cols = [jnp.where(ck + c * LANES <= rq, s, NEG_INF) for c, s in enumerate(cols)]
            mx = functools.reduce(jnp.maximum, cols)
            m_old = m_ref[hd, rows]
            m_new = jnp.maximum(m_old, jnp.max(mx, axis=1, keepdims=True))
            alpha = jnp.exp2(m_old - m_new)
            ps = [jnp.exp2(s - m_new) for s in cols]
            row = functools.reduce(lambda a, b: a + b, ps)
            l_ref[hd, rows] = alpha * l_ref[hd, rows] + jnp.sum(row, axis=1, keepdims=True)
            m_ref[hd, rows] = m_new
            for c, p in enumerate(ps):
                p_buf[slot, hd, rows, c * LANES:(c + 1) * LANES] = p.astype(BF16)
            if pv is not None:
                acc_ref[hd, rows] = alpha * (acc_ref[hd, rows] + pv[rows])

    def values(hd, j, slot):
        v = v_ref[hd, pl.ds(pl.multiple_of(j * t, t), t), :]
        return jnp.dot(p_buf[slot, hd], v, preferred_element_type=F32)

    def step(j, slot, *, ahead, behind, diagonal):
        for hd in range(nh):
            pv = values(hd, j - 1, 1 - slot) if behind else None
            if ahead:
                scores(hd, j + 1, 1 - slot)
            softmax(hd, slot, diagonal, pv)

    def finish(j, slot):
        for hd in range(nh):
            acc = acc_ref[hd] + values(hd, j, slot)
            o_ref[:, hd * LANES:(hd + 1) * LANES] = (acc / l_ref[hd]).astype(o_ref.dtype)

    m_ref[...] = jnp.full(m_ref.shape, NEG_INF, F32)
    l_ref[...] = jnp.zeros(l_ref.shape, F32)
    acc_ref[...] = jnp.zeros(acc_ref.shape, F32)
    for hd in range(nh):
        scores(hd, 0, 0)

    @pl.when(n == 1)
    def _():
        step(0, 0, ahead=False, behind=False, diagonal=True)
        finish(0, 0)

    @pl.when(n > 1)
    def _():
        step(0, 0, ahead=True, behind=False, diagonal=False)

        def pair(j):
            step(j, 1, ahead=True, behind=True, diagonal=False)
            step(j + 1, 0, ahead=True, behind=True, diagonal=False)

        def body(u, carry):
            for k in range(ATTN_PAIRS):
                pair(2 * ATTN_PAIRS * u + 2 * k + 1)
            return carry

        bodies = lax.div(n - 2, 2 * ATTN_PAIRS)
        lax.fori_loop(0, bodies, body, 0)
        left = lax.div(lax.rem(n - 2, 2 * ATTN_PAIRS), 2)
        for k in range(ATTN_PAIRS - 1):
            @pl.when(left > k)
            def _():
                pair(2 * ATTN_PAIRS * bodies + 2 * k + 1)

        @pl.when(lax.rem(n, 2) == 1)
        def _():
            step(n - 2, 1, ahead=True, behind=True, diagonal=False)
            step(n - 1, 0, ahead=False, behind=True, diagonal=True)
            finish(n - 1, 0)

        @pl.when(lax.rem(n, 2) == 0)
        def _():
            step(n - 1, 1, ahead=False, behind=True, diagonal=True)
            finish(n - 1, 1)


def _attention(q8, k8, v8, t, nh):
    heads, s, _ = q8.shape
    resident = lambda w: pl.BlockSpec((nh, s, w), lambda h, i: (h, 0, 0),
                                      pipeline_mode=pl.Buffered(1))
    return pl.pallas_call(
        functools.partial(_attn_kernel, t=t, nh=nh),
        grid=(heads // nh, s // t),
        in_specs=[pl.BlockSpec((nh, t, QK_PAD), lambda h, i: (h, i, 0)),
                  resident(QK_PAD), resident(LANES)],
        out_specs=pl.BlockSpec((t, nh * LANES), lambda h, i: (i, h)),
        out_shape=jax.ShapeDtypeStruct((s, heads * LANES), BF16),
        scratch_shapes=[pltpu.VMEM((2, nh, t, t), F32), pltpu.VMEM((2, nh, t, t), BF16),
                        pltpu.VMEM((nh, t, LANES), F32), pltpu.VMEM((nh, t, LANES), F32),
                        pltpu.VMEM((nh, t, LANES), F32)],
        compiler_params=_cparams(("parallel", "arbitrary")),
        name="attention",
    )(q8, k8, v8)


def _layer_norm(y, g, b):
    mu = jnp.mean(y, axis=-1, keepdims=True)
    yc = y - mu
    var = jnp.mean(yc * yc, axis=-1, keepdims=True)
    return yc * lax.rsqrt(var + NORM_EPS) * g + b


def _out_proj_kernel(att_ref, mix_ref, x_ref, wa_ref, wm_ref, g_ref, b_ref, o_ref):
    y = jnp.dot(att_ref[...], wa_ref[...], preferred_element_type=F32)
    y = y + jnp.dot(mix_ref[...], wm_ref[...], preferred_element_type=F32)
    y = y + DEEPNORM_ALPHA * x_ref[...]
    o_ref[...] = _layer_norm(y, g_ref[...], b_ref[...])


def _out_proj(att, mix, x, w_att, w_mix, ln_g, ln_b, tm):
    s, d = x.shape
    row = lambda w: pl.BlockSpec((tm, w), lambda i: (i, 0))
    full = lambda a: pl.BlockSpec(a.shape, lambda i: (0,) * a.ndim)
    return pl.pallas_call(
        _out_proj_kernel,
        grid=(s // tm,),
        in_specs=[row(att.shape[1]), row(mix.shape[1]), row(d), full(w_att), full(w_mix),
                  full(ln_g), full(ln_b)],
        out_specs=row(d),
        out_shape=jax.ShapeDtypeStruct((s, d), F32),
        compiler_params=_cparams(("parallel",)),
        name="out_proj_ln",
    )(att, mix, x, w_att, w_mix, ln_g, ln_b)


def _ffn_kernel(x_ref, wg_ref, wu_ref, wd_ref, g_ref, b_ref, o_ref, xb_ref):
    f = pl.program_id(1)

    @pl.when(f == 0)
    def _():
        x = x_ref[...]
        xb_ref[...] = x.astype(BF16)
        o_ref[...] = DEEPNORM_ALPHA * x

    xb = xb_ref[...]
    gate = jnp.dot(xb, wg_ref[...], preferred_element_type=F32)
    up = jnp.dot(xb, wu_ref[...], preferred_element_type=F32)
    act = (gate * jax.nn.sigmoid(gate) * up).astype(BF16)
    o_ref[...] += jnp.dot(act, wd_ref[...], preferred_element_type=F32)

    @pl.when(f == pl.num_programs(1) - 1)
    def _():
        o_ref[...] = _layer_norm(o_ref[...], g_ref[...], b_ref[...])


def _ffn(x, w_gate, w_up, w_down, ln_g, ln_b, layer, tm, tf):
    s, d = x.shape
    dff = w_gate.shape[2]
    return pl.pallas_call(
        _ffn_kernel,
        grid=(s // tm, dff // tf),
        in_specs=[pl.BlockSpec((tm, d), lambda i, f: (i, 0)),
                  pl.BlockSpec((None, d, tf), lambda i, f: (layer, 0, f)),
                  pl.BlockSpec((None, d, tf), lambda i, f: (layer, 0, f)),
                  pl.BlockSpec((None, tf, d), lambda i, f: (layer, f, 0)),
                  pl.BlockSpec((1, d), lambda i, f: (0, 0)),
                  pl.BlockSpec((1, d), lambda i, f: (0, 0))],
        out_specs=pl.BlockSpec((tm, d), lambda i, f: (i, 0)),
        out_shape=jax.ShapeDtypeStruct((s, d), F32),
        scratch_shapes=[pltpu.VMEM((tm, d), BF16)],
        compiler_params=_cparams(("parallel", "arbitrary")),
        name="ffn_ln",
    )(x, w_gate, w_up, w_down, ln_g, ln_b)


def _moba_permute(w):
    d = w.shape[0]
    w = w.reshape(d, MOBA_HEADS, MOBA_DIM)
    half = MOBA_ROT // 2
    out = jnp.concatenate([w[:, :, :half], w[:, :, MOBA_ROT:MOBA_ROT + 48],
                           w[:, :, half:MOBA_ROT], w[:, :, MOBA_ROT + 48:]], axis=2)
    return out.reshape(d, W_GROUP)


def _relayout_w_in(w_in):
    d = w_in.shape[0]
    c_q, c_kv, k_rope, mq, mk, mv, conv_in, pool_in = jnp.split(
        w_in, [512, 768, 832, 1344, 1856, 2368, 3392], axis=1)
    permute = _moba_permute
    z32 = jnp.zeros((d, 32), w_in.dtype)
    kr = jnp.concatenate([k_rope[:, :32], z32, k_rope[:, 32:], z32], axis=1)
    pad = jnp.zeros((d, IN_PAD - COL_KROPE - LANES), w_in.dtype)
    return jnp.concatenate([conv_in[:, :512], conv_in[:, 512:], c_q, pool_in,
                            permute(mq), permute(mk), mv, c_kv, kr, pad], axis=1)


def _relayout_w_uq(w_uq):
    r = w_uq.shape[0]
    w = w_uq.reshape(r, MLA_HEADS, MLA_QK)
    z32 = jnp.zeros((r, MLA_HEADS, 32), w_uq.dtype)
    half = MLA_ROPE // 2
    out = jnp.concatenate([w[:, :, :MLA_NOPE], w[:, :, MLA_NOPE:MLA_NOPE + half], z32,
                           w[:, :, MLA_NOPE + half:], z32], axis=2)
    return out.reshape(r, MLA_HEADS * QK_PAD)


def _rope_tables(positions):
    pos = positions.astype(F32)[0][:, None]

    def cos_sin(dim):
        inv_freq = 1.0 / (ROPE_THETA ** (jnp.arange(0, dim, 2, dtype=F32) / dim))
        ang = (pos * inv_freq).reshape(-1, LANES)
        return jnp.cos(ang).reshape(-1, dim // 2), jnp.sin(ang).reshape(-1, dim // 2)

    s = pos.shape[0]
    c, sn = cos_sin(MLA_ROPE)
    z = jnp.zeros((s, 32), F32)
    ca = jnp.concatenate([c, z, c, z], axis=1)
    sa = jnp.concatenate([-sn, z, sn, z], axis=1)
    c, sn = cos_sin(MOBA_ROT)
    one = jnp.ones((s, 48), F32)
    z = jnp.zeros((s, 48), F32)
    cm = jnp.concatenate([c, one, c, one], axis=1)
    sm = jnp.concatenate([-sn, z, sn, z], axis=1)
    return ca, sa, cm, sm


def kernel(x, positions, w_in, g_q, w_uq, g_kv, w_ukv, conv_dw_w, conv_dw_b, conv_ln_g,
           conv_ln_b, conv_pw_w, pool_w, pool_scale, w_out, ln1_g, ln1_b, w_gate, w_up,
           w_down, ln2_g, ln2_b):
    b, s, d = x.shape
    assert b == 1 and d == D_MODEL and s % 1024 == 0
    ca, sa, cm, sm = _rope_tables(positions)
    row2 = lambda a: a.reshape(1, -1)
    w_gate_b, w_up_b, w_down_b = w_gate.astype(BF16), w_up.astype(BF16), w_down.astype(BF16)
    h = x[0]
    for l in range(DEPTH):
        w_in_p = _relayout_w_in(w_in[l]).astype(BF16)
        w_uq_p = _relayout_w_uq(w_uq[l]).astype(BF16)
        w_att = w_out[l][:2 * W_GROUP].astype(BF16)
        w_mix = w_out[l][2 * W_GROUP:].astype(BF16)
        dw_w = jnp.pad(conv_dw_w[l], ((0, 1), (0, 0)))

        proj = _matmul(h, w_in_p, F32, tm=1024, tn=1024)
        q8, k8, v8 = _prep_attn(proj, row2(g_q[l]), row2(g_kv[l]), w_uq_p,
                                w_ukv[l].astype(BF16), ca, sa, cm, sm, tr=256)
        mix = _prep_mix(proj, dw_w, row2(conv_dw_b[l]), row2(conv_ln_g[l]),
                        row2(conv_ln_b[l]), conv_pw_w[l].astype(BF16),
                        pool_w[l].astype(BF16), row2(pool_scale[l]), tr=256)
        att = _attention(q8, k8, v8, t=512, nh=2)
        x1 = _out_proj(att, mix, h, w_att, w_mix, row2(ln1_g[l]), row2(ln1_b[l]), tm=512)
        h = _ffn(x1, w_gate_b, w_up_b, w_down_b, row2(ln2_g[l]), row2(ln2_b[l]), layer=l,
                 tm=512, tf=512)
    return h[None]
```
